```python
import math
import jax, jax.numpy as jnp
from jax import lax
import numpy as np


D_MODEL = 1024
BATCH = 4
SEQ = 4096
DEPTH = 4
DEC_BATCH = 8
DEC_SEQ = 8192
PAST_LEN = 128

N_MIXERS = 2
N_A_LAYERS = (DEPTH + 1) // 2
N_B_LAYERS = DEPTH // 2
CHUNK = 128
GMLP_WIDTH = D_MODEL
GMLP_GROUPS = 8
GMLP_GROUP_DIM = GMLP_WIDTH // GMLP_GROUPS
DIFF_HEADS = 8
DIFF_HEAD_DIM = D_MODEL // (2 * DIFF_HEADS)
DIFF_V_DIM = 2 * DIFF_HEAD_DIM
D_FF = 2816
CONV_WIDTH = 3
ROPE_THETA = 10000.0
NORM_EPS = 1e-6
SUBLN_EPS = 1e-5
BLOCK_Q = 128

kernel_name = "hybrid_gmlp_diffattn_convffn_encoder"


def rmsnorm(x, g, eps=NORM_EPS):
    xf = x.astype(jnp.float32)
    y = xf * lax.rsqrt(jnp.mean(xf * xf, axis=-1, keepdims=True) + eps)
    return (y * g.astype(jnp.float32)).astype(x.dtype)


def rope(x):
    S, dh = x.shape[1], x.shape[-1]
    pos = jnp.arange(S, dtype=jnp.float32)
    inv_freq = ROPE_THETA ** (-jnp.arange(0, dh, 2, dtype=jnp.float32) / dh)
    ang = pos[:, None] * inv_freq[None, :]
    ang = jnp.concatenate([ang, ang], axis=-1)
    cos = jnp.cos(ang)[None, :, None, :].astype(x.dtype)
    sin = jnp.sin(ang)[None, :, None, :].astype(x.dtype)
    x1, x2 = jnp.split(x, 2, axis=-1)
    rot = jnp.concatenate([-x2, x1], axis=-1)
    return x * cos + rot * sin


def gmlp_mixer(h, w_in, v_gain, w_s, b_s, w_out):
    B, S, _ = h.shape
    z = jax.nn.gelu(h @ w_in)
    u, v = jnp.split(z, 2, axis=-1)
    v = rmsnorm(v, v_gain)
    v = v.reshape(B, S // CHUNK, CHUNK, GMLP_GROUPS, GMLP_GROUP_DIM)
    v = jnp.einsum('gpq,bcqgd->bcpgd', w_s, v) + b_s.T[None, None, :, :, None]
    y = u * v.reshape(B, S, GMLP_WIDTH)
    return y @ w_out


def diff_attention(h, w_qkv, lam_q1, lam_k1, lam_q2, lam_k2, subln_g, w_out, layer_idx):
    B, S, _ = h.shape
    H, dh = DIFF_HEADS, DIFF_HEAD_DIM
    q, k, v = jnp.split(h @ w_qkv, 3, axis=-1)
    q = rope(q.reshape(B, S, 2 * H, dh)).reshape(B, S, H, 2, dh)
    k = rope(k.reshape(B, S, 2 * H, dh)).reshape(B, S, H, 2, dh)
    v = v.reshape(B, S, H, DIFF_V_DIM)
    scale = dh ** -0.5
    lam_init = 0.8 - 0.6 * math.exp(-0.3 * layer_idx)
    f32 = jnp.float32
    lam = (jnp.exp(jnp.sum(lam_q1.astype(f32) * lam_k1.astype(f32)))
           - jnp.exp(jnp.sum(lam_q2.astype(f32) * lam_k2.astype(f32))) + lam_init)
    nb = S // BLOCK_Q
    qb = jnp.moveaxis(q.reshape(B, nb, BLOCK_Q, H, 2, dh), 1, 0)

    def block(qblk):
        s = jnp.einsum('bqhcd,bkhcd->bhcqk', qblk, k).astype(f32) * scale
        p = jax.nn.softmax(s, axis=-1)
        a = p[:, :, 0] - lam * p[:, :, 1]
        return jnp.einsum('bhqk,bkhe->bqhe', a.astype(v.dtype), v)

    o = lax.map(block, qb)
    o = jnp.moveaxis(o, 0, 1).reshape(B, S, H, DIFF_V_DIM)
    o = rmsnorm(o, subln_g, SUBLN_EPS) * (1.0 - lam_init)
    return o.reshape(B, S, H * DIFF_V_DIM) @ w_out


def conv_ffn(h, w_in, conv_w, conv_b, w_out):
    S = h.shape[1]
    a = h @ w_in
    half = CONV_WIDTH // 2
    ap = jnp.pad(a, ((0, 0), (half, half), (0, 0)))
    c = conv_b
    for t in range(CONV_WIDTH):
        c = c + ap[:, t:t + S] * conv_w[t]
    g, u = jnp.split(c, 2, axis=-1)
    return (jax.nn.silu(g) * u) @ w_out


def trunk(x, norm_mix, norm_ffn, norm_final,
          gmlp_w_in, gmlp_v_gain, gmlp_w_s, gmlp_b_s, gmlp_w_out,
          diff_w_qkv, diff_lam_q1, diff_lam_k1, diff_lam_q2, diff_lam_k2, diff_subln_g, diff_w_out,
          ffn_w_in, ffn_conv_w, ffn_conv_b, ffn_w_out):
    for i in range(DEPTH):
        h = rmsnorm(x, norm_mix[i])
        j = i // N_MIXERS
        if i % N_MIXERS == 0:
            x = x + gmlp_mixer(h, gmlp_w_in[j], gmlp_v_gain[j], gmlp_w_s[j], gmlp_b_s[j], gmlp_w_out[j])
        else:
            x = x + diff_attention(h, diff_w_qkv[j], diff_lam_q1[j], diff_lam_k1[j],
                                   diff_lam_q2[j], diff_lam_k2[j], diff_subln_g[j], diff_w_out[j], i)
        h = rmsnorm(x, norm_ffn[i])
        x = x + conv_ffn(h, ffn_w_in[i], ffn_conv_w[i], ffn_conv_b[i], ffn_w_out[i])
    return rmsnorm(x, norm_final)


def setup_inputs(seed: int = 0) -> dict:
    key = jax.random.key(seed)
    ks = jax.random.split(key, 24)
    f32 = jnp.float32
    nrm = lambda k, shape, s: (jax.random.normal(k, shape, f32) * s)
    res_scale = (2.0 * DEPTH) ** -0.5
    return {
        'x_prompt': nrm(ks[0], (BATCH, SEQ, D_MODEL), 1.0),
        'x_sample': nrm(ks[1], (DEC_BATCH, DEC_SEQ, D_MODEL), 1.0),
        'norm_mix': 1.0 + nrm(ks[2], (DEPTH, D_MODEL), 0.02),
        'norm_ffn': 1.0 + nrm(ks[3], (DEPTH, D_MODEL), 0.02),
        'norm_final': 1.0 + nrm(ks[4], (D_MODEL,), 0.02),
        'gmlp_w_in': nrm(ks[5], (N_A_LAYERS, D_MODEL, 2 * GMLP_WIDTH), D_MODEL ** -0.5),
        'gmlp_v_gain': 1.0 + nrm(ks[6], (N_A_LAYERS, GMLP_WIDTH), 0.02),
        'gmlp_w_s': nrm(ks[7], (N_A_LAYERS, GMLP_GROUPS, CHUNK, CHUNK), CHUNK ** -0.5),
        'gmlp_b_s': 1.0 + nrm(ks[8], (N_A_LAYERS, GMLP_GROUPS, CHUNK), 0.02),
        'gmlp_w_out': nrm(ks[9], (N_A_LAYERS, GMLP_WIDTH, D_MODEL), GMLP_WIDTH ** -0.5 * res_scale),
        'diff_w_qkv': nrm(ks[10], (N_B_LAYERS, D_MODEL, 3 * D_MODEL), D_MODEL ** -0.5),
        'diff_lam_q1': nrm(ks[11], (N_B_LAYERS, DIFF_HEAD_DIM), 0.1),
        'diff_lam_k1': nrm(ks[12], (N_B_LAYERS, DIFF_HEAD_DIM), 0.1),
        'diff_lam_q2': nrm(ks[13], (N_B_LAYERS, DIFF_HEAD_DIM), 0.1),
        'diff_lam_k2': nrm(ks[14], (N_B_LAYERS, DIFF_HEAD_DIM), 0.1),
        'diff_subln_g': 1.0 + nrm(ks[15], (N_B_LAYERS, DIFF_V_DIM), 0.02),
        'diff_w_out': nrm(ks[16], (N_B_LAYERS, D_MODEL, D_MODEL), D_MODEL ** -0.5 * res_scale),
        'ffn_w_in': nrm(ks[17], (DEPTH, D_MODEL, 2 * D_FF), D_MODEL ** -0.5),
        'ffn_conv_w': nrm(ks[18], (DEPTH, CONV_WIDTH, 2 * D_FF), CONV_WIDTH ** -0.5),
        'ffn_conv_b': nrm(ks[19], (DEPTH, 2 * D_FF), 0.01),
        'ffn_w_out': nrm(ks[20], (DEPTH, D_FF, D_MODEL), D_FF ** -0.5 * res_scale),
    }


def reference(x_prompt, x_sample, norm_mix, norm_ffn, norm_final,
              gmlp_w_in, gmlp_v_gain, gmlp_w_s, gmlp_b_s, gmlp_w_out,
              diff_w_qkv, diff_lam_q1, diff_lam_k1, diff_lam_q2, diff_lam_k2, diff_subln_g, diff_w_out,
              ffn_w_in, ffn_conv_w, ffn_conv_b, ffn_w_out):
    params = (norm_mix, norm_ffn, norm_final,
              gmlp_w_in, gmlp_v_gain, gmlp_w_s, gmlp_b_s, gmlp_w_out,
              diff_w_qkv, diff_lam_q1, diff_lam_k1, diff_lam_q2, diff_lam_k2, diff_subln_g, diff_w_out,
              ffn_w_in, ffn_conv_w, ffn_conv_b, ffn_w_out)
    y_prompt = trunk(x_prompt, *params)
    y_sample = trunk(x_sample, *params)
    return (y_prompt, y_sample)
```

```python
import functools
import math

import jax
import jax.numpy as jnp
from jax import lax
from jax.experimental import pallas as pl
from jax.experimental.pallas import tpu as pltpu

CHUNK = 128
GMLP_GROUPS = 8
DIFF_HEADS = 8
DIFF_HEAD_DIM = 64
DIFF_V_DIM = 2 * DIFF_HEAD_DIM
CONV_WIDTH = 3
ROPE_THETA = 10000.0
NORM_EPS = 1e-6
SUBLN_EPS = 1e-5

LANES = 128
SUBLANES = 8
MXU_WIDTH = 256
ROW_TILE = 512
Q_TILE = 512
K_TILE = 512
FFN_CHUNK = MXU_WIDTH
VMEM_LIMIT = 56 * 1024 * 1024

F32 = jnp.float32
BF16 = jnp.bfloat16


def _dot(a, b):
    return jnp.dot(a, b, preferred_element_type=F32)


def _rms(x, g, eps):
    ms = jnp.mean(x * x, axis=-1, keepdims=True)
    return x * lax.rsqrt(ms + eps) * g


def _const_spec(shape):
    zeros = (0,) * len(shape)
    return pl.BlockSpec(shape, lambda *_: zeros, pipeline_mode=pl.Buffered(1))


def _params(n_grid):
    return pltpu.CompilerParams(
        dimension_semantics=("arbitrary",) * n_grid, vmem_limit_bytes=VMEM_LIMIT)


def _gmlp_kernel(x_ref, g_ref, win_ref, vg_ref, ws_ref, bs_ref, wout_ref, o_ref, y_ref,
                 *, n_chunks, width):
    x = x_ref[...]
    hb = _rms(x, g_ref[...], NORM_EPS).astype(BF16)
    z = _dot(hb, win_ref[...])
    z = 0.5 * z * (1.0 + jnp.tanh(math.sqrt(2.0 / math.pi) * (z + 0.044715 * (z * z * z))))
    u = z[:, :width]
    v = _rms(z[:, width:], vg_ref[...], NORM_EPS).astype(BF16)
    for g in range(GMLP_GROUPS):
        cols = slice(g * CHUNK, (g + 1) * CHUNK)
        vcat = jnp.concatenate(
            [v[c * CHUNK:(c + 1) * CHUNK, cols] for c in range(n_chunks)], axis=1)
        r = _dot(ws_ref[g], vcat)
        b = bs_ref[:, cols]
        for c in range(n_chunks):
            rows = slice(c * CHUNK, (c + 1) * CHUNK)
            y_ref[rows, cols] = (u[rows, cols] * (r[:, c * CHUNK:(c + 1) * CHUNK] + b)).astype(BF16)
    o_ref[...] = x + _dot(y_ref[...], wout_ref[...])


def _gmlp_layer(x, g_mix, w_in, v_gain, w_s, b_full, w_out):
    B, S, D = x.shape
    W = w_out.shape[0]
    T = min(ROW_TILE, S)
    assert S % T == 0 and T % CHUNK == 0
    row = pl.BlockSpec((None, T, D), lambda b, i: (b, i, 0))
    return pl.pallas_call(
        functools.partial(_gmlp_kernel, n_chunks=T // CHUNK, width=W),
        grid=(B, S // T),
        in_specs=[row, _const_spec((1, D)), _const_spec((D, 2 * W)), _const_spec((1, W)),
                  _const_spec((GMLP_GROUPS, CHUNK, CHUNK)), _const_spec((CHUNK, W)),
                  _const_spec((W, D))],
        out_specs=row,
        out_shape=jax.ShapeDtypeStruct((B, S, D), F32),
        scratch_shapes=[pltpu.VMEM((T, W), BF16)],
        compiler_params=_params(2),
        name="gmlp_layer",
    )(x, g_mix, w_in, v_gain, w_s, b_full, w_out)


def _qkv_kernel(x_ref, g_ref, wqv_ref, wk_ref, cq_ref, sq_ref, ck_ref, sk_ref,
                qa_ref, qb_ref, k_ref, vt_ref, *, d_model):
    hb = _rms(x_ref[...], g_ref[...], NORM_EPS).astype(BF16)
    qv_t = lax.dot_general(wqv_ref[...], hb, (((1,), (1,)), ((), ())),
                           preferred_element_type=F32)
    kk = _dot(hb, wk_ref[...])
    cq, sq = cq_ref[...], sq_ref[...]
    ck, sk = ck_ref[...], sk_ref[...]
    scale = DIFF_HEAD_DIM ** -0.5
    half = DIFF_HEAD_DIM // 2
    zero = jnp.zeros((half, hb.shape[0]), F32)
    for h in range(DIFF_HEADS):
        xh = qv_t[h * DIFF_V_DIM:(h + 1) * DIFF_V_DIM]
        x1, x2 = xh[:DIFF_HEAD_DIM], xh[DIFF_HEAD_DIM:]
        o1 = (x1 * cq - x2 * sq) * scale
        o2 = (x2 * cq + x1 * sq) * scale
        qa_ref[h] = jnp.concatenate([o1[:half], zero, o2[:half], zero], axis=0).astype(BF16)
        qb_ref[h] = jnp.concatenate([zero, o1[half:], zero, o2[half:]], axis=0).astype(BF16)
        vt_ref[h, 0] = qv_t[d_model + h * DIFF_V_DIM:d_model + (h + 1) * DIFF_V_DIM].astype(BF16)
        kh = kk[:, h * DIFF_V_DIM:(h + 1) * DIFF_V_DIM]
        k_ref[h] = (kh * ck + pltpu.roll(kh, DIFF_HEAD_DIM, 1) * sk).astype(BF16)


def _qkv_rope(x, g_mix, wqv_t, wk, tables, T):
    B, S, D = x.shape
    H = DIFF_HEADS
    cq, sq, ck, sk = tables
    hd = DIFF_V_DIM
    q_spec = pl.BlockSpec((None, H, hd, T), lambda b, i: (b, 0, 0, i))
    return pl.pallas_call(
        functools.partial(_qkv_kernel, d_model=D),
        grid=(B, S // T),
        in_specs=[pl.BlockSpec((None, T, D), lambda b, i: (b, i, 0)),
                  _const_spec((1, D)), _const_spec((2 * D, D)), _const_spec((D, D)),
                  pl.BlockSpec((DIFF_HEAD_DIM, T), lambda b, i: (0, i)),
                  pl.BlockSpec((DIFF_HEAD_DIM, T), lambda b, i: (0, i)),
                  pl.BlockSpec((T, hd), lambda b, i: (i, 0)),
                  pl.BlockSpec((T, hd), lambda b, i: (i, 0))],
        out_specs=[q_spec, q_spec,
                   pl.BlockSpec((None, H, T, hd), lambda b, i: (b, 0, i, 0)),
                   pl.BlockSpec((None, H, 1, hd, T), lambda b, i: (b, 0, i, 0, 0))],
        out_shape=[jax.ShapeDtypeStruct((B, H, hd, S), BF16),
                   jax.ShapeDtypeStruct((B, H, hd, S), BF16),
                   jax.ShapeDtypeStruct((B, H, S, hd), BF16),
                   jax.ShapeDtypeStruct((B, H, S // T, hd, T), BF16)],
        compiler_params=_params(2),
        name="qkv_rope",
    )(x, g_mix, wqv_t, wk, cq, sq, ck, sk)


def _attn_kernel(qa_ref, qb_ref, k_ref, vt_ref, lq1_ref, lk1_ref, lq2_ref, lk2_ref, g_ref,
                 o_ref, acc_ref, *, tq, tk, lam_init):
    q2 = jnp.concatenate([qa_ref[...], qb_ref[...]], axis=1)
    acc_ref[...] = jnp.zeros_like(acc_ref)
    n_k = k_ref.shape[0] // tk

    def body(kb, carry):
        m, l = carry
        k = k_ref[pl.ds(pl.multiple_of(kb * tk, tk), tk), :]
        s = _dot(k, q2)
        m_new = jnp.maximum(m, jnp.max(s, axis=0, keepdims=True))
        alpha = jnp.exp(m - m_new)
        p = jnp.exp(s - m_new)
        l = alpha * l + jnp.sum(p, axis=0, keepdims=True)
        acc_ref[...] = acc_ref[...] * alpha + _dot(vt_ref[kb], p.astype(BF16))
        return m_new, l

    m0 = jnp.full((1, 2 * tq), -jnp.inf, F32)
    _, l = lax.fori_loop(0, n_k, body, (m0, jnp.zeros((1, 2 * tq), F32)))
    acc = acc_ref[...]
    lam = (jnp.exp(jnp.sum(lq1_ref[...] * lk1_ref[...], axis=1, keepdims=True))
           - jnp.exp(jnp.sum(lq2_ref[...] * lk2_ref[...], axis=1, keepdims=True)) + lam_init)
    o = acc[:, :tq] / l[:, :tq] - lam * (acc[:, tq:] / l[:, tq:])
    ms = jnp.mean(o * o, axis=0, keepdims=True)
    o = o * lax.rsqrt(ms + SUBLN_EPS) * g_ref[...] * (1.0 - lam_init)
    o_ref[...] = o.T.astype(BF16)


def _diff_attn(qa, qb, k, vt, lam_q1, lam_k1, lam_q2, lam_k2, subln_g, lam_init, tq):
    B, H, hd, S = qa.shape
    n_kb, tk = vt.shape[2], vt.shape[4]
    q_spec = pl.BlockSpec((None, None, hd, tq), lambda b, h, i: (b, h, 0, i))
    lam_spec = _const_spec((1, DIFF_HEAD_DIM))
    return pl.pallas_call(
        functools.partial(_attn_kernel, tq=tq, tk=tk, lam_init=lam_init),
        grid=(B, H, S // tq),
        in_specs=[q_spec, q_spec,
                  pl.BlockSpec((None, None, S, hd), lambda b, h, i: (b, h, 0, 0)),
                  pl.BlockSpec((None, None, n_kb, hd, tk), lambda b, h, i: (b, h, 0, 0, 0)),
                  lam_spec, lam_spec, lam_spec, lam_spec, _const_spec((hd, 1))],
        out_specs=pl.BlockSpec((None, tq, hd), lambda b, h, i: (b, i, h)),
        out_shape=jax.ShapeDtypeStruct((B, S, H * hd), BF16),
        scratch_shapes=[pltpu.VMEM((hd, 2 * tq), F32)],
        compiler_params=_params(3),
        name="diff_attn",
    )(qa, qb, k, vt, lam_q1, lam_k1, lam_q2, lam_k2, subln_g)


def _ffn_kernel(*refs, tile, n_chunk, has_proj, final_norm):
    refs = list(refs)
    x_ref, xp_ref, xn_ref = refs[:3]
    del refs[:3]
    if has_proj:
        o_ref, op_ref, on_ref, wo_ref = refs[:4]
        del refs[:4]
    gn_ref, win_ref, cw_ref, cb_ref, wout_ref = refs[:5]
    del refs[:5]
    if final_norm:
        gf_ref = refs.pop(0)
    out_ref, acc_ref = refs

    halo = SUBLANES
    rows = tile + 2 * halo
    i, n = pl.program_id(1), pl.num_programs(1)
    xa = jnp.concatenate([xp_ref[...], x_ref[...], xn_ref[...]], axis=0)
    if has_proj:
        oa = jnp.concatenate([op_ref[...], o_ref[...], on_ref[...]], axis=0)
        xa = xa + _dot(oa, wo_ref[...])
    h = _rms(xa, gn_ref[...], NORM_EPS)
    r = lax.broadcasted_iota(jnp.int32, (rows, 1), 0)
    inside = ((r >= halo) | (i > 0)) & ((r < tile + halo) | (i < n - 1))
    hb = jnp.where(inside, h, 0.0).astype(BF16)
    acc_ref[...] = xa[halo:halo + tile]
    for j in range(n_chunk):
        a = _dot(hb, win_ref[j])
        cw = cw_ref[j]
        a_prev = pltpu.roll(a, 1, 0)[halo:halo + tile]
        a_next = pltpu.roll(a, rows - 1, 0)[halo:halo + tile]
        c = cb_ref[j] + a_prev * cw[0:1] + a[halo:halo + tile] * cw[1:2] + a_next * cw[2:3]
        gate, up = c[:, :FFN_CHUNK], c[:, FFN_CHUNK:]
        y = (gate * (1.0 / (1.0 + jnp.exp(-gate))) * up).astype(BF16)
        acc_ref[...] += _dot(y, wout_ref[j])
    out = acc_ref[...]
    if final_norm:
        out = _rms(out, gf_ref[...], NORM_EPS)
    out_ref[...] = out


def _conv_ffn(x, g_ffn, w_in, conv_w, conv_b, w_out, proj=None, g_final=None):
    B, S, D = x.shape
    T = min(ROW_TILE, S)
    halo = SUBLANES
    n_chunk = w_in.shape[0]
    assert S % T == 0 and T % halo == 0
    tb, last = T // halo, S // halo - 1

    def triple(dtype_unused=None):
        return [pl.BlockSpec((None, T, D), lambda b, i: (b, i, 0)),
                pl.BlockSpec((None, halo, D), lambda b, i: (b, jnp.maximum(i * tb - 1, 0), 0)),
                pl.BlockSpec((None, halo, D), lambda b, i: (b, jnp.minimum((i + 1) * tb, last), 0))]

    args, specs = [x, x, x], triple()
    if proj is not None:
        o, w_o = proj
        args += [o, o, o, w_o]
        specs += triple() + [_const_spec(w_o.shape)]
    args += [g_ffn, w_in, conv_w, conv_b, w_out]
    specs += [_const_spec((1, D)), _const_spec(w_in.shape), _const_spec(conv_w.shape),
              _const_spec(conv_b.shape), _const_spec(w_out.shape)]
    if g_final is not None:
        args.append(g_final)
        specs.append(_const_spec((1, D)))
    return pl.pallas_call(
        functools.partial(_ffn_kernel, tile=T, n_chunk=n_chunk, has_proj=proj is not None,
                          final_norm=g_final is not None),
        grid=(B, S // T),
        in_specs=specs,
        out_specs=pl.BlockSpec((None, T, D), lambda b, i: (b, i, 0)),
        out_shape=jax.ShapeDtypeStruct((B, S, D), F32),
        scratch_shapes=[pltpu.VMEM((T, D), F32)],
        compiler_params=_params(2),
        name="conv_ffn",
    )(*args)


def _rope_tables(S):
    half = DIFF_HEAD_DIM // 2
    pos = jnp.arange(S, dtype=F32)
    inv_freq = ROPE_THETA ** (-jnp.arange(0, DIFF_HEAD_DIM, 2, dtype=F32) / DIFF_HEAD_DIM)
    ang = pos[:, None] * inv_freq[None, :]
    cos, sin = jnp.cos(ang), jnp.sin(ang)
    cq = jnp.concatenate([cos, cos], axis=1).T
    sq = jnp.concatenate([sin, sin], axis=1).T
    ck = jnp.concatenate([cos] * 4, axis=1)
    sk = jnp.concatenate([-sin, -sin, sin, sin], axis=1)
    return cq, sq, ck, sk


def _head_feature_order():
    half = DIFF_HEAD_DIM // 2
    n = jnp.arange(DIFF_V_DIM)
    part, c, i = n // DIFF_HEAD_DIM, (n % DIFF_HEAD_DIM) // half, n % half
    within = c * DIFF_HEAD_DIM + part * half + i
    return (jnp.arange(DIFF_HEADS)[:, None] * DIFF_V_DIM + within[None, :]).reshape(-1)


def _prep_ffn(w_in, conv_w, conv_b, w_out):
    D, F2 = w_in.shape
    F = F2 // 2
    assert F % FFN_CHUNK == 0
    n = F // FFN_CHUNK

    def pair(a):
        g = a[..., :F].reshape(a.shape[:-1] + (n, FFN_CHUNK))
        u = a[..., F:].reshape(a.shape[:-1] + (n, FFN_CHUNK))
        return jnp.moveaxis(jnp.concatenate([g, u], axis=-1), -2, 0)

    return (pair(w_in).astype(BF16), pair(conv_w), pair(conv_b[None, :]),
            w_out.reshape(n, FFN_CHUNK, D).astype(BF16))


def _trunk(x, p):
    B, S, D = x.shape
    depth = p["norm_mix"].shape[0]
    T = min(ROW_TILE, S)
    tables = _rope_tables(S)
    pending = None
    for i in range(depth):
        j = i // 2
        g_mix = p["norm_mix"][i][None, :]
        if i % 2 == 0:
            x = _gmlp_layer(x, g_mix, *p["gmlp"][j])
        else:
            wqv_t, wk, w_o, lam, subln_g = p["diff"][j]
            qa, qb, k, vt = _qkv_rope(x, g_mix, wqv_t, wk, tables, min(K_TILE, S))
            lam_init = 0.8 - 0.6 * math.exp(-0.3 * i)
            o = _diff_attn(qa, qb, k, vt, *lam, subln_g, lam_init, min(Q_TILE, S))
            pending = (o, w_o)
        g_final = p["norm_final"][None, :] if i == depth - 1 else None
        x = _conv_ffn(x, p["norm_ffn"][i][None, :], *p["ffn"][i], proj=pending, g_final=g_final)
        pending = None
    return x


def kernel(x_prompt, x_sample, norm_mix, norm_ffn, norm_final, gmlp_w_in, gmlp_v_gain, gmlp_w_s, gmlp_b_s, gmlp_w_out, diff_w_qkv, diff_lam_q1, diff_lam_k1, diff_lam_q2, diff_lam_k2, diff_subln_g, diff_w_out, ffn_w_in, ffn_conv_w, ffn_conv_b, ffn_w_out):
    D = x_prompt.shape[-1]
    order = _head_feature_order()
    gmlp = []
    for j in range(gmlp_w_in.shape[0]):
        b_full = jnp.repeat(gmlp_b_s[j].T, CHUNK, axis=1)
        gmlp.append((gmlp_w_in[j].astype(BF16), gmlp_v_gain[j][None, :], gmlp_w_s[j].astype(BF16),
                     b_full, gmlp_w_out[j].astype(BF16)))
    diff = []
    for j in range(diff_w_qkv.shape[0]):
        w = diff_w_qkv[j]
        wq, wk, wv = w[:, :D][:, order], w[:, D:2 * D][:, order], w[:, 2 * D:]
        wqv_t = jnp.concatenate([wq, wv], axis=1).T.astype(BF16)
        lam = tuple(a[j][None, :] for a in (diff_lam_q1, diff_lam_k1, diff_lam_q2, diff_lam_k2))
        diff.append((wqv_t, wk.astype(BF16), diff_w_out[j].astype(BF16), lam,
                     diff_subln_g[j][:, None]))
    ffn = [_prep_ffn(ffn_w_in[i], ffn_conv_w[i], ffn_conv_b[i], ffn_w_out[i])
           for i in range(ffn_w_in.shape[0])]
    p = dict(norm_mix=norm_mix, norm_ffn=norm_ffn, norm_final=norm_final,
             gmlp=gmlp, diff=diff, ffn=ffn)
    return (_trunk(x_prompt, p), _trunk(x_sample, p))
```

```python
import functools
import math

import jax
import jax.numpy as jnp
from jax import lax
from jax.experimental import pallas as pl
from jax.experimental.pallas import tpu as pltpu

CHUNK = 128
GMLP_GROUPS = 8
DIFF_HEADS = 8
DIFF_HEAD_DIM = 64
DIFF_V_DIM = 2 * DIFF_HEAD_DIM
CONV_WIDTH = 3
ROPE_THETA = 10000.0
NORM_EPS = 1e-6
SUBLN_EPS = 1e-5

LANES = 128
SUBLANES = 8
MXU_WIDTH = 256
ROW_TILE = 512
Q_TILE = 1024
K_TILE = 512
SOFTMAX_ROWS = 64
SUM_ROWS = 16
FFN_CHUNK = MXU_WIDTH
VMEM_LIMIT = 56 * 1024 * 1024

F32 = jnp.float32
BF16 = jnp.bfloat16


def _dot(a, b):
    return jnp.dot(a, b, preferred_element_type=F32)


def _rms(x, g, eps):
    ms = jnp.mean(x * x, axis=-1, keepdims=True)
    return x * lax.rsqrt(ms + eps) * g


def _const_spec(shape):
    zeros = (0,) * len(shape)
    return pl.BlockSpec(shape, lambda *_: zeros, pipeline_mode=pl.Buffered(1))


def _params(n_grid, flags=None):
    return pltpu.CompilerParams(
        dimension_semantics=("arbitrary",) * n_grid, vmem_limit_bytes=VMEM_LIMIT, flags=flags)


def _gmlp_kernel(x_ref, g_ref, win_ref, vg_ref, ws_ref, bs_ref, wout_ref, o_ref, y_ref,
                 *, n_chunks, width):
    x = x_ref[...]
    hb = _rms(x, g_ref[...], NORM_EPS).astype(BF16)
    z = _dot(hb, win_ref[...])
    z = 0.5 * z * (1.0 + jnp.tanh(math.sqrt(2.0 / math.pi) * (z + 0.044715 * (z * z * z))))
    u = z[:, :width]
    v = _rms(z[:, width:], vg_ref[...], NORM_EPS).astype(BF16)
    for g in range(GMLP_GROUPS):
        cols = slice(g * CHUNK, (g + 1) * CHUNK)
        vcat = jnp.concatenate(
            [v[c * CHUNK:(c + 1) * CHUNK, cols] for c in range(n_chunks)], axis=1)
        r = _dot(ws_ref[g], vcat)
        b = bs_ref[:, cols]
        for c in range(n_chunks):
            rows = slice(c * CHUNK, (c + 1) * CHUNK)
            y_ref[rows, cols] = (u[rows, cols] * (r[:, c * CHUNK:(c + 1) * CHUNK] + b)).astype(BF16)
    o_ref[...] = x + _dot(y_ref[...], wout_ref[...])


def _gmlp_layer(x, g_mix, w_in, v_gain, w_s, b_full, w_out):
    B, S, D = x.shape
    W = w_out.shape[0]
    T = min(ROW_TILE, S)
    assert S % T == 0 and T % CHUNK == 0
    row = pl.BlockSpec((None, T, D), lambda b, i: (b, i, 0))
    return pl.pallas_call(
        functools.partial(_gmlp_kernel, n_chunks=T // CHUNK, width=W),
        grid=(B, S // T),
        in_specs=[row, _const_spec((1, D)), _const_spec((D, 2 * W)), _const_spec((1, W)),
                  _const_spec((GMLP_GROUPS, CHUNK, CHUNK)), _const_spec((CHUNK, W)),
                  _const_spec((W, D))],
        out_specs=row,
        out_shape=jax.ShapeDtypeStruct((B, S, D), F32),
        scratch_shapes=[pltpu.VMEM((T, W), BF16)],
        compiler_params=_params(2),
        name="gmlp_layer",
    )(x, g_mix, w_in, v_gain, w_s, b_full, w_out)


def _qkv_kernel(x_ref, g_ref, wqv_ref, wk_ref, cq_ref, sq_ref, ck_ref, sk_ref,
                qa_ref, qb_ref, k_ref, vt_ref, *, d_model):
    hb = _rms(x_ref[...], g_ref[...], NORM_EPS).astype(BF16)
    qv_t = lax.dot_general(wqv_ref[...], hb, (((1,), (1,)), ((), ())),
                           preferred_element_type=F32)
    kk = _dot(hb, wk_ref[...])
    cq, sq = cq_ref[...], sq_ref[...]
    ck, sk = ck_ref[...], sk_ref[...]
    scale = DIFF_HEAD_DIM ** -0.5 * math.log2(math.e)
    half = DIFF_HEAD_DIM // 2
    zero = jnp.zeros((half, hb.shape[0]), F32)
    ones = jnp.ones((SUM_ROWS, hb.shape[0]), F32)
    for h in range(DIFF_HEADS):
        xh = qv_t[h * DIFF_V_DIM:(h + 1) * DIFF_V_DIM]
        x1, x2 = xh[:DIFF_HEAD_DIM], xh[DIFF_HEAD_DIM:]
        o1 = (x1 * cq - x2 * sq) * scale
        o2 = (x2 * cq + x1 * sq) * scale
        qa_ref[h] = jnp.concatenate([o1[:half], zero, o2[:half], zero], axis=0).astype(BF16)
        qb_ref[h] = jnp.concatenate([zero, o1[half:], zero, o2[half:]], axis=0).astype(BF16)
        vt_ref[h, 0] = jnp.concatenate(
            [qv_t[d_model + h * DIFF_V_DIM:d_model + (h + 1) * DIFF_V_DIM], ones], axis=0).astype(BF16)
        kh = kk[:, h * DIFF_V_DIM:(h + 1) * DIFF_V_DIM]
        k_ref[h] = (kh * ck + pltpu.roll(kh, DIFF_HEAD_DIM, 1) * sk).astype(BF16)


def _qkv_rope(x, g_mix, wqv_t, wk, tables, T):
    B, S, D = x.shape
    H = DIFF_HEADS
    cq, sq, ck, sk = tables
    hd = DIFF_V_DIM
    q_spec = pl.BlockSpec((None, H, hd, T), lambda b, i: (b, 0, 0, i))
    return pl.pallas_call(
        functools.partial(_qkv_kernel, d_model=D),
        grid=(B, S // T),
        in_specs=[pl.BlockSpec((None, T, D), lambda b, i: (b, i, 0)),
                  _const_spec((1, D)), _const_spec((2 * D, D)), _const_spec((D, D)),
                  pl.BlockSpec((DIFF_HEAD_DIM, T), lambda b, i: (0, i)),
                  pl.BlockSpec((DIFF_HEAD_DIM, T), lambda b, i: (0, i)),
                  pl.BlockSpec((T, hd), lambda b, i: (i, 0)),
                  pl.BlockSpec((T, hd), lambda b, i: (i, 0))],
        out_specs=[q_spec, q_spec,
                   pl.BlockSpec((None, H, T, hd), lambda b, i: (b, 0, i, 0)),
                   pl.BlockSpec((None, H, 1, hd + SUM_ROWS, T), lambda b, i: (b, 0, i, 0, 0))],
        out_shape=[jax.ShapeDtypeStruct((B, H, hd, S), BF16),
                   jax.ShapeDtypeStruct((B, H, hd, S), BF16),
                   jax.ShapeDtypeStruct((B, H, S, hd), BF16),
                   jax.ShapeDtypeStruct((B, H, S // T, hd + SUM_ROWS, T), BF16)],
        compiler_params=_params(2),
        name="qkv_rope",
    )(x, g_mix, wqv_t, wk, cq, sq, ck, sk)


def _attn_kernel(qa_ref, qb_ref, k_ref, vt_ref, lq1_ref, lk1_ref, lq2_ref, lk2_ref, g_ref,
                 o_ref, sa_ref, sb_ref, pa_ref, pb_ref, acca_ref, accb_ref, *, tk, lam_init):
    qa, qb = qa_ref[...], qb_ref[...]
    tq = qa.shape[1]
    n_k = k_ref.shape[0] // tk

    def k_block(i):
        return k_ref[pl.ds(pl.multiple_of(i * tk, tk), tk), :]

    def col_max(s):
        part = jnp.max(s.reshape(s.shape[0] // SUBLANES, SUBLANES, tq), axis=0)
        return jnp.max(part, axis=0, keepdims=True)

    def scores(i, q, s_ref):
        s = _dot(k_block(i), q)
        s_ref[...] = s
        return col_max(s)

    def softmax_step(s_ref, p_ref, m_blk, m):
        m_new = jnp.maximum(m, m_blk)
        for r in range(0, tk, SOFTMAX_ROWS):
            p_ref[r:r + SOFTMAX_ROWS, :] = jnp.exp2(s_ref[r:r + SOFTMAX_ROWS, :] - m_new).astype(BF16)
        return m_new, jnp.exp2(m - m_new)

    def half_step(j, u, carry):
        mxa, ma, mb, alpha_b = carry
        w = 1 - u
        mxb = scores(j, qb, sb_ref.at[u])
        accb_ref[...] = accb_ref[...] * alpha_b + _dot(vt_ref[jnp.maximum(j - 1, 0)], pb_ref[w])
        ma, alpha_a = softmax_step(sa_ref.at[u], pa_ref.at[u], mxa, ma)
        mxa = scores(jnp.minimum(j + 1, n_k - 1), qa, sa_ref.at[w])
        acca_ref[...] = acca_ref[...] * alpha_a + _dot(vt_ref[j], pa_ref[u])
        mb, alpha_b = softmax_step(sb_ref.at[u], pb_ref.at[u], mxb, mb)
        return mxa, ma, mb, alpha_b

    mxa0 = scores(0, qa, sa_ref.at[0])
    pb_ref[1] = jnp.zeros((tk, tq), BF16)
    acca_ref[...] = jnp.zeros_like(acca_ref)
    accb_ref[...] = jnp.zeros_like(accb_ref)
    neg = jnp.full((1, tq), -jnp.inf, F32)
    _, _, _, alpha_b = lax.fori_loop(
        0, n_k // 2, lambda i, c: half_step(2 * i + 1, 1, half_step(2 * i, 0, c)),
        (mxa0, neg, neg, jnp.ones((1, tq), F32)))
    acc_a = acca_ref[...]
    acc_b = accb_ref[...] * alpha_b + _dot(vt_ref[n_k - 1], pb_ref[1])
    hd = DIFF_V_DIM
    la, lb = acc_a[hd:hd + 1], acc_b[hd:hd + 1]
    lam = (jnp.exp(jnp.sum(lq1_ref[...] * lk1_ref[...], axis=1, keepdims=True))
           - jnp.exp(jnp.sum(lq2_ref[...] * lk2_ref[...], axis=1, keepdims=True)) + lam_init)
    o = acc_a[:hd] / la - lam * (acc_b[:hd] / lb)
    ms = jnp.mean(o * o, axis=0, keepdims=True)
    o = o * lax.rsqrt(ms + SUBLN_EPS) * g_ref[...] * (1.0 - lam_init)
    o_ref[...] = o.T.astype(BF16)


def _diff_attn(qa, qb, k, vt, lam_q1, lam_k1, lam_q2, lam_k2, subln_g, lam_init, tq):
    B, H, hd, S = qa.shape
    n_kb, vt_rows, tk = vt.shape[2:]
    assert n_kb % 2 == 0 and S % tq == 0
    q_spec = pl.BlockSpec((None, None, hd, tq), lambda b, h, i: (b, h, 0, i))
    lam_spec = _const_spec((1, DIFF_HEAD_DIM))
    return pl.pallas_call(
        functools.partial(_attn_kernel, tk=tk, lam_init=lam_init),
        grid=(B, H, S // tq),
        in_specs=[q_spec, q_spec,
                  pl.BlockSpec((None, None, S, hd), lambda b, h, i: (b, h, 0, 0)),
                  pl.BlockSpec((None, None, n_kb, vt_rows, tk), lambda b, h, i: (b, h, 0, 0, 0)),
                  lam_spec, lam_spec, lam_spec, lam_spec, _const_spec((hd, 1))],
        out_specs=pl.BlockSpec((None, tq, hd), lambda b, h, i: (b, i, h)),
        out_shape=jax.ShapeDtypeStruct((B, S, H * hd), BF16),
        scratch_shapes=[pltpu.VMEM((2, tk, tq), F32), pltpu.VMEM((2, tk, tq), F32),
                        pltpu.VMEM((2, tk, tq), BF16), pltpu.VMEM((2, tk, tq), BF16),
                        pltpu.VMEM((vt_rows, tq), F32), pltpu.VMEM((vt_rows, tq), F32)],
        compiler_params=_params(3),
        name="diff_attn",
    )(qa, qb, k, vt, lam_q1, lam_k1, lam_q2, lam_k2, subln_g)


def _ffn_kernel(*refs, tile, n_chunk, has_proj, final_norm):
    refs = list(refs)
    x_ref, xp_ref, xn_ref = refs[:3]
    del refs[:3]
    if has_proj:
        o_ref, op_ref, on_ref, wo_ref = refs[:4]
        del refs[:4]
    gn_ref, win_ref, cw_ref, cb_ref, wout_ref = refs[:5]
    del refs[:5]
    if final_norm:
        gf_ref = refs.pop(0)
    out_ref, acc_ref = refs

    halo = SUBLANES
    rows = tile + 2 * halo
    i, n = pl.program_id(1), pl.num_programs(1)
    xa = jnp.concatenate([xp_ref[...], x_ref[...], xn_ref[...]], axis=0)
    if has_proj:
        oa = jnp.concatenate([op_ref[...], o_ref[...], on_ref[...]], axis=0)
        xa = xa + _dot(oa, wo_ref[...])
    h = _rms(xa, gn_ref[...], NORM_EPS)
    r = lax.broadcasted_iota(jnp.int32, (rows, 1), 0)
    inside = ((r >= halo) | (i > 0)) & ((r < tile + halo) | (i < n - 1))
    hb = jnp.where(inside, h, 0.0).astype(BF16)
    acc_ref[...] = xa[halo:halo + tile]
    for j in range(n_chunk):
        a = _dot(hb, win_ref[j])
        cw = cw_ref[j]
        a_prev = pltpu.roll(a, 1, 0)[halo:halo + tile]
        a_next = pltpu.roll(a, rows - 1, 0)[halo:halo + tile]
        c = cb_ref[j] + a_prev * cw[0:1] + a[halo:halo + tile] * cw[1:2] + a_next * cw[2:3]
        gate, up = c[:, :FFN_CHUNK], c[:, FFN_CHUNK:]
        y = (gate * (1.0 / (1.0 + jnp.exp(-gate))) * up).astype(BF16)
        acc_ref[...] += _dot(y, wout_ref[j])
    out = acc_ref[...]
    if final_norm:
        out = _rms(out, gf_ref[...], NORM_EPS)
    out_ref[...] = out


def _conv_ffn(x, g_ffn, w_in, conv_w, conv_b, w_out, proj=None, g_final=None):
    B, S, D = x.shape
    T = min(ROW_TILE, S)
    halo = SUBLANES
    n_chunk = w_in.shape[0]
    assert S % T == 0 and T % halo == 0
    tb, last = T // halo, S // halo - 1

    def triple(dtype_unused=None):
        return [pl.BlockSpec((None, T, D), lambda b, i: (b, i, 0)),
                pl.BlockSpec((None, halo, D), lambda b, i: (b, jnp.maximum(i * tb - 1, 0), 0)),
                pl.BlockSpec((None, halo, D), lambda b, i: (b, jnp.minimum((i + 1) * tb, last), 0))]

    args, specs = [x, x, x], triple()
    if proj is not None:
        o, w_o = proj
        args += [o, o, o, w_o]
        specs += triple() + [_const_spec(w_o.shape)]
    args += [g_ffn, w_in, conv_w, conv_b, w_out]
    specs += [_const_spec((1, D)), _const_spec(w_in.shape), _const_spec(conv_w.shape),
              _const_spec(conv_b.shape), _const_spec(w_out.shape)]
    if g_final is not None:
        args.append(g_final)
        specs.append(_const_spec((1, D)))
    return pl.pallas_call(
        functools.partial(_ffn_kernel, tile=T, n_chunk=n_chunk, has_proj=proj is not None,
                          final_norm=g_final is not None),
        grid=(B, S // T),
        in_specs=specs,
        out_specs=pl.BlockSpec((None, T, D), lambda b, i: (b, i, 0)),
        out_shape=jax.ShapeDtypeStruct((B, S, D), F32),
        scratch_shapes=[pltpu.VMEM((T, D), F32)],
        compiler_params=_params(2),
        name="conv_ffn",
    )(*args)


def _rope_tables(S):
    half = DIFF_HEAD_DIM // 2
    pos = jnp.arange(S, dtype=F32)
    inv_freq = ROPE_THETA ** (-jnp.arange(0, DIFF_HEAD_DIM, 2, dtype=F32) / DIFF_HEAD_DIM)
    ang = pos[:, None] * inv_freq[None, :]
    cos, sin = jnp.cos(ang), jnp.sin(ang)
    cq = jnp.concatenate([cos, cos], axis=1).T
    sq = jnp.concatenate([sin, sin], axis=1).T
    ck = jnp.concatenate([cos] * 4, axis=1)
    sk = jnp.concatenate([-sin, -sin, sin, sin], axis=1)
    return cq, sq, ck, sk


def _head_feature_order():
    half = DIFF_HEAD_DIM // 2
    n = jnp.arange(DIFF_V_DIM)
    part, c, i = n // DIFF_HEAD_DIM, (n % DIFF_HEAD_DIM) // half, n % half
    within = c * DIFF_HEAD_DIM + part * half + i
    return (jnp.arange(DIFF_HEADS)[:, None] * DIFF_V_DIM + within[None, :]).reshape(-1)


def _prep_ffn(w_in, conv_w, conv_b, w_out):
    D, F2 = w_in.shape
    F = F2 // 2
    assert F % FFN_CHUNK == 0
    n = F // FFN_CHUNK

    def pair(a):
        g = a[..., :F].reshape(a.shape[:-1] + (n, FFN_CHUNK))
        u = a[..., F:].reshape(a.shape[:-1] + (n, FFN_CHUNK))
        return jnp.moveaxis(jnp.concatenate([g, u], axis=-1), -2, 0)

    return (pair(w_in).astype(BF16), pair(conv_w), pair(conv_b[None, :]),
            w_out.reshape(n, FFN_CHUNK, D).astype(BF16))


def _trunk(x, p):
    B, S, D = x.shape
    depth = p["norm_mix"].shape[0]
    T = min(ROW_TILE, S)
    tables = _rope_tables(S)
    pending = None
    for i in range(depth):
        j = i // 2
        g_mix = p["norm_mix"][i][None, :]
        if i % 2 == 0:
            x = _gmlp_layer(x, g_mix, *p["gmlp"][j])
        else:
            wqv_t, wk, w_o, lam, subln_g = p["diff"][j]
            qa, qb, k, vt = _qkv_rope(x, g_mix, wqv_t, wk, tables, min(K_TILE, S))
            lam_init = 0.8 - 0.6 * math.exp(-0.3 * i)
            o = _diff_attn(qa, qb, k, vt, *lam, subln_g, lam_init, min(Q_TILE, S))
            pending = (o, w_o)
        g_final = p["norm_final"][None, :] if i == depth - 1 else None
        x = _conv_ffn(x, p["norm_ffn"][i][None, :], *p["ffn"][i], proj=pending, g_final=g_final)
        pending = None
    return x


def kernel(x_prompt, x_sample, norm_mix, norm_ffn, norm_final, gmlp_w_in, gmlp_v_gain, gmlp_w_s, gmlp_b_s, gmlp_w_out, diff_w_qkv, diff_lam_q1, diff_lam_k1, diff_lam_q2, diff_lam_k2, diff_subln_g, diff_w_out, ffn_w_in, ffn_conv_w, ffn_conv_b, ffn_w_out):
    D = x_prompt.shape[-1]
    order = _head_feature_order()
    gmlp = []
    for j in range(gmlp_w_in.shape[0]):
        b_full = jnp.repeat(gmlp_b_s[j].T, CHUNK, axis=1)
        gmlp.append((gmlp_w_in[j].astype(BF16), gmlp_v_gain[j][None, :], gmlp_w_s[j].astype(BF16),
                     b_full, gmlp_w_out[j].astype(BF16)))
    diff = []
    for j in range(diff_w_qkv.shape[0]):
        w = diff_w_qkv[j]
        wq, wk, wv = w[:, :D][:, order], w[:, D:2 * D][:, order], w[:, 2 * D:]
        wqv_t = jnp.concatenate([wq, wv], axis=1).T.astype(BF16)
        lam = tuple(a[j][None, :] for a in (diff_lam_q1, diff_lam_k1, diff_lam_q2, diff_lam_k2))
        diff.append((wqv_t, wk.astype(BF16), diff_w_out[j].astype(BF16), lam,
                     diff_subln_g[j][:, None]))
    ffn = [_prep_ffn(ffn_w_in[i], ffn_conv_w[i], ffn_conv_b[i], ffn_w_out[i])
           for i in range(ffn_w_in.shape[0])]
    p = dict(norm_mix=norm_mix, norm_ffn=norm_ffn, norm_final=norm_final,
             gmlp=gmlp, diff=diff, ffn=ffn)
    return (_trunk(x_prompt, p), _trunk(x_sample, p))
```

```python
import functools
import math

import jax
import jax.numpy as jnp
from jax import lax
from jax.experimental import pallas as pl
from jax.experimental.pallas import tpu as pltpu

CHUNK = 128
GMLP_GROUPS = 8
DIFF_HEADS = 8
DIFF_HEAD_DIM = 64
DIFF_V_DIM = 2 * DIFF_HEAD_DIM
CONV_WIDTH = 3
ROPE_THETA = 10000.0
NORM_EPS = 1e-6
SUBLN_EPS = 1e-5

LANES = 128
SUBLANES = 8
MXU_WIDTH = 256
ROW_TILE = 512
Q_TILE = 1024
K_TILE = 512
SOFTMAX_ROWS = 64
SUM_ROWS = 16
FFN_CHUNK = MXU_WIDTH
VMEM_LIMIT = 56 * 1024 * 1024

F32 = jnp.float32
BF16 = jnp.bfloat16


def _dot(a, b):
    return jnp.dot(a, b, preferred_element_type=F32)


def _rms(x, g, eps):
    ms = jnp.mean(x * x, axis=-1, keepdims=True)
    return x * lax.rsqrt(ms + eps) * g


def _const_spec(shape):
    zeros = (0,) * len(shape)
    return pl.BlockSpec(shape, lambda *_: zeros, pipeline_mode=pl.Buffered(1))


def _params(n_grid, flags=None):
    return pltpu.CompilerParams(
        dimension_semantics=("arbitrary",) * n_grid, vmem_limit_bytes=VMEM_LIMIT, flags=flags)


def _gmlp_kernel(x_ref, g_ref, win_ref, vg_ref, ws_ref, bs_ref, wout_ref, o_ref, y_ref,
                 *, n_chunks, width):
    x = x_ref[...]
    hb = _rms(x, g_ref[...], NORM_EPS).astype(BF16)
    z = _dot(hb, win_ref[...])
    z = 0.5 * z * (1.0 + jnp.tanh(math.sqrt(2.0 / math.pi) * (z + 0.044715 * (z * z * z))))
    u = z[:, :width]
    v = _rms(z[:, width:], vg_ref[...], NORM_EPS).astype(BF16)
    for g in range(GMLP_GROUPS):
        cols = slice(g * CHUNK, (g + 1) * CHUNK)
        vcat = jnp.concatenate(
            [v[c * CHUNK:(c + 1) * CHUNK, cols] for c in range(n_chunks)], axis=1)
        r = _dot(ws_ref[g], vcat)
        b = bs_ref[:, cols]
        for c in range(n_chunks):
            rows = slice(c * CHUNK, (c + 1) * CHUNK)
            y_ref[rows, cols] = (u[rows, cols] * (r[:, c * CHUNK:(c + 1) * CHUNK] + b)).astype(BF16)
    o_ref[...] = x + _dot(y_ref[...], wout_ref[...])


def _gmlp_layer(x, g_mix, w_in, v_gain, w_s, b_full, w_out):
    B, S, D = x.shape
    W = w_out.shape[0]
    T = min(ROW_TILE, S)
    assert S % T == 0 and T % CHUNK == 0
    row = pl.BlockSpec((None, T, D), lambda b, i: (b, i, 0))
    return pl.pallas_call(
        functools.partial(_gmlp_kernel, n_chunks=T // CHUNK, width=W),
        grid=(B, S // T),
        in_specs=[row, _const_spec((1, D)), _const_spec((D, 2 * W)), _const_spec((1, W)),
                  _const_spec((GMLP_GROUPS, CHUNK, CHUNK)), _const_spec((CHUNK, W)),
                  _const_spec((W, D))],
        out_specs=row,
        out_shape=jax.ShapeDtypeStruct((B, S, D), F32),
        scratch_shapes=[pltpu.VMEM((T, W), BF16)],
        compiler_params=_params(2),
        name="gmlp_layer",
    )(x, g_mix, w_in, v_gain, w_s, b_full, w_out)


def _qkv_kernel(x_ref, g_ref, wqv_ref, wk_ref, cq_ref, sq_ref, ck_ref, sk_ref,
                qa_ref, qb_ref, k_ref, vt_ref, *, d_model):
    hb = _rms(x_ref[...], g_ref[...], NORM_EPS).astype(BF16)
    qv_t = lax.dot_general(wqv_ref[...], hb, (((1,), (1,)), ((), ())),
                           preferred_element_type=F32)
    kk = _dot(hb, wk_ref[...])
    cq, sq = cq_ref[...], sq_ref[...]
    ck, sk = ck_ref[...], sk_ref[...]
    scale = DIFF_HEAD_DIM ** -0.5 * math.log2(math.e)
    half = DIFF_HEAD_DIM // 2
    zero = jnp.zeros((half, hb.shape[0]), F32)
    ones = jnp.ones((SUM_ROWS, hb.shape[0]), F32)
    for h in range(DIFF_HEADS):
        xh = qv_t[h * DIFF_V_DIM:(h + 1) * DIFF_V_DIM]
        x1, x2 = xh[:DIFF_HEAD_DIM], xh[DIFF_HEAD_DIM:]
        o1 = (x1 * cq - x2 * sq) * scale
        o2 = (x2 * cq + x1 * sq) * scale
        qa_ref[h] = jnp.concatenate([o1[:half], zero, o2[:half], zero], axis=0).astype(BF16)
        qb_ref[h] = jnp.concatenate([zero, o1[half:], zero, o2[half:]], axis=0).astype(BF16)
        vt_ref[h, 0] = jnp.concatenate(
            [qv_t[d_model + h * DIFF_V_DIM:d_model + (h + 1) * DIFF_V_DIM], ones], axis=0).astype(BF16)
        kh = kk[:, h * DIFF_V_DIM:(h + 1) * DIFF_V_DIM]
        k_ref[h] = (kh * ck + pltpu.roll(kh, DIFF_HEAD_DIM, 1) * sk).astype(BF16)


def _qkv_rope(x, g_mix, wqv_t, wk, tables, T):
    B, S, D = x.shape
    H = DIFF_HEADS
    cq, sq, ck, sk = tables
    hd = DIFF_V_DIM
    q_spec = pl.BlockSpec((None, H, hd, T), lambda b, i: (b, 0, 0, i))
    return pl.pallas_call(
        functools.partial(_qkv_kernel, d_model=D),
        grid=(B, S // T),
        in_specs=[pl.BlockSpec((None, T, D), lambda b, i: (b, i, 0)),
                  _const_spec((1, D)), _const_spec((2 * D, D)), _const_spec((D, D)),
                  pl.BlockSpec((DIFF_HEAD_DIM, T), lambda b, i: (0, i)),
                  pl.BlockSpec((DIFF_HEAD_DIM, T), lambda b, i: (0, i)),
                  pl.BlockSpec((T, hd), lambda b, i: (i, 0)),
                  pl.BlockSpec((T, hd), lambda b, i: (i, 0))],
        out_specs=[q_spec, q_spec,
                   pl.BlockSpec((None, H, T, hd), lambda b, i: (b, 0, i, 0)),
                   pl.BlockSpec((None, H, 1, hd + SUM_ROWS, T), lambda b, i: (b, 0, i, 0, 0))],
        out_shape=[jax.ShapeDtypeStruct((B, H, hd, S), BF16),
                   jax.ShapeDtypeStruct((B, H, hd, S), BF16),
                   jax.ShapeDtypeStruct((B, H, S, hd), BF16),
                   jax.ShapeDtypeStruct((B, H, S // T, hd + SUM_ROWS, T), BF16)],
        compiler_params=_params(2),
        name="qkv_rope",
    )(x, g_mix, wqv_t, wk, cq, sq, ck, sk)


def _attn_kernel(qa_ref, qb_ref, k_ref, vt_ref, lq1_ref, lk1_ref, lq2_ref, lk2_ref, g_ref,
                 o_ref, sa_ref, sb_ref, pa_ref, pb_ref, acca_ref, accb_ref, *, tk, lam_init):
    qa, qb = qa_ref[...], qb_ref[...]
    tq = qa.shape[1]
    n_k = k_ref.shape[0] // tk

    def k_block(i):
        return k_ref[pl.ds(pl.multiple_of(i * tk, tk), tk), :]

    def col_max(s):
        part = jnp.max(s.reshape(s.shape[0] // SUBLANES, SUBLANES, tq), axis=0)
        return jnp.max(part, axis=0, keepdims=True)

    def scores(i, q, s_ref):
        s = _dot(k_block(i), q)
        s_ref[...] = s
        return col_max(s)

    def softmax_step(s_ref, p_ref, m_blk, m):
        m_new = jnp.maximum(m, m_blk)
        for r in range(0, tk, SOFTMAX_ROWS):
            p_ref[r:r + SOFTMAX_ROWS, :] = jnp.exp2(s_ref[r:r + SOFTMAX_ROWS, :] - m_new).astype(BF16)
        return m_new, jnp.exp2(m - m_new)

    def half_step(j, u, carry):
        mxa, ma, mb, alpha_b = carry
        w = 1 - u
        mxb = scores(j, qb, sb_ref.at[u])
        accb_ref[...] = accb_ref[...] * alpha_b + _dot(vt_ref[jnp.maximum(j - 1, 0)], pb_ref[w])
        ma, alpha_a = softmax_step(sa_ref.at[u], pa_ref.at[u], mxa, ma)
        mxa = scores(jnp.minimum(j + 1, n_k - 1), qa, sa_ref.at[w])
        acca_ref[...] = acca_ref[...] * alpha_a + _dot(vt_ref[j], pa_ref[u])
        mb, alpha_b = softmax_step(sb_ref.at[u], pb_ref.at[u], mxb, mb)
        return mxa, ma, mb, alpha_b

    mxa0 = scores(0, qa, sa_ref.at[0])
    pb_ref[1] = jnp.zeros((tk, tq), BF16)
    acca_ref[...] = jnp.zeros_like(acca_ref)
    accb_ref[...] = jnp.zeros_like(accb_ref)
    neg = jnp.full((1, tq), -jnp.inf, F32)
    _, _, _, alpha_b = lax.fori_loop(
        0, n_k // 2, lambda i, c: half_step(2 * i + 1, 1, half_step(2 * i, 0, c)),
        (mxa0, neg, neg, jnp.ones((1, tq), F32)))
    acc_a = acca_ref[...]
    acc_b = accb_ref[...] * alpha_b + _dot(vt_ref[n_k - 1], pb_ref[1])
    hd = DIFF_V_DIM
    la, lb = acc_a[hd:hd + 1], acc_b[hd:hd + 1]
    lam = (jnp.exp(jnp.sum(lq1_ref[...] * lk1_ref[...], axis=1, keepdims=True))
           - jnp.exp(jnp.sum(lq2_ref[...] * lk2_ref[...], axis=1, keepdims=True)) + lam_init)
    o = acc_a[:hd] / la - lam * (acc_b[:hd] / lb)
    ms = jnp.mean(o * o, axis=0, keepdims=True)
    o = o * lax.rsqrt(ms + SUBLN_EPS) * g_ref[...] * (1.0 - lam_init)
    o_ref[...] = o.T.astype(BF16)


def _diff_attn(qa, qb, k, vt, lam_q1, lam_k1, lam_q2, lam_k2, subln_g, lam_init, tq):
    B, H, hd, S = qa.shape
    n_kb, vt_rows, tk = vt.shape[2:]
    assert n_kb % 2 == 0 and S % tq == 0
    q_spec = pl.BlockSpec((None, None, hd, tq), lambda b, h, i: (b, h, 0, i))
    lam_spec = _const_spec((1, DIFF_HEAD_DIM))
    return pl.pallas_call(
        functools.partial(_attn_kernel, tk=tk, lam_init=lam_init),
        grid=(B, H, S // tq),
        in_specs=[q_spec, q_spec,
                  pl.BlockSpec((None, None, S, hd), lambda b, h, i: (b, h, 0, 0)),
                  pl.BlockSpec((None, None, n_kb, vt_rows, tk), lambda b, h, i: (b, h, 0, 0, 0)),
                  lam_spec, lam_spec, lam_spec, lam_spec, _const_spec((hd, 1))],
        out_specs=pl.BlockSpec((None, tq, hd), lambda b, h, i: (b, i, h)),
        out_shape=jax.ShapeDtypeStruct((B, S, H * hd), BF16),
        scratch_shapes=[pltpu.VMEM((2, tk, tq), F32), pltpu.VMEM((2, tk, tq), F32),
                        pltpu.VMEM((2, tk, tq), BF16), pltpu.VMEM((2, tk, tq), BF16),
                        pltpu.VMEM((vt_rows, tq), F32), pltpu.VMEM((vt_rows, tq), F32)],
        compiler_params=_params(3),
        name="diff_attn",
    )(qa, qb, k, vt, lam_q1, lam_k1, lam_q2, lam_k2, subln_g)


def _ffn_kernel(*refs, tile, n_chunk, has_proj, final_norm):
    refs = list(refs)
    x_ref, xp_ref, xn_ref = refs[:3]
    del refs[:3]
    if has_proj:
        o_ref, op_ref, on_ref, wo_ref = refs[:4]
        del refs[:4]
    gn_ref, win_ref, cw_ref, cb_ref, wout_ref = refs[:5]
    del refs[:5]
    if final_norm:
        gf_ref = refs.pop(0)
    out_ref, y_ref = refs

    halo = SUBLANES
    rows = tile + 2 * halo
    i, n = pl.program_id(1), pl.num_programs(1)
    xa = jnp.concatenate([xp_ref[...], x_ref[...], xn_ref[...]], axis=0)
    if has_proj:
        oa = jnp.concatenate([op_ref[...], o_ref[...], on_ref[...]], axis=0)
        xa = xa + _dot(oa, wo_ref[...])
    h = _rms(xa, gn_ref[...], NORM_EPS)
    r = lax.broadcasted_iota(jnp.int32, (rows, 1), 0)
    inside = ((r >= halo) | (i > 0)) & ((r < tile + halo) | (i < n - 1))
    hb = jnp.where(inside, h, 0.0).astype(BF16)
    n_vr = tile // SUBLANES
    sub = lax.broadcasted_iota(jnp.int32, (1, SUBLANES, 1), 1)
    for j in range(n_chunk):
        a = _dot(hb, win_ref[j])
        cw = cw_ref[j]
        a3 = a.reshape(n_vr + 2, SUBLANES, 2 * FFN_CHUNK)
        dn = pltpu.roll(a3, 1, 1)
        up = pltpu.roll(a3, SUBLANES - 1, 1)
        a_prev = jnp.where(sub == 0, dn[0:n_vr], dn[1:n_vr + 1])
        a_next = jnp.where(sub == SUBLANES - 1, up[2:n_vr + 2], up[1:n_vr + 1])
        c = cb_ref[j] + a_prev * cw[0:1] + a3[1:n_vr + 1] * cw[1:2] + a_next * cw[2:3]
        c = c.reshape(tile, 2 * FFN_CHUNK)
        gate, lin = c[:, :FFN_CHUNK], c[:, FFN_CHUNK:]
        y_ref[:, j * FFN_CHUNK:(j + 1) * FFN_CHUNK] = (
            gate * (1.0 / (1.0 + jnp.exp(-gate))) * lin).astype(BF16)
    out = xa[halo:halo + tile] + _dot(y_ref[...], wout_ref[...])
    if final_norm:
        out = _rms(out, gf_ref[...], NORM_EPS)
    out_ref[...] = out


def _conv_ffn(x, g_ffn, w_in, conv_w, conv_b, w_out, proj=None, g_final=None):
    B, S, D = x.shape
    T = min(ROW_TILE, S)
    halo = SUBLANES
    n_chunk = w_in.shape[0]
    assert S % T == 0 and T % halo == 0
    tb, last = T // halo, S // halo - 1

    def triple():
        return [pl.BlockSpec((None, T, D), lambda b, i: (b, i, 0)),
                pl.BlockSpec((None, halo, D), lambda b, i: (b, jnp.maximum(i * tb - 1, 0), 0)),
                pl.BlockSpec((None, halo, D), lambda b, i: (b, jnp.minimum((i + 1) * tb, last), 0))]

    args, specs = [x, x, x], triple()
    if proj is not None:
        o, w_o = proj
        args += [o, o, o, w_o]
        specs += triple() + [_const_spec(w_o.shape)]
    args += [g_ffn, w_in, conv_w, conv_b, w_out]
    specs += [_const_spec((1, D)), _const_spec(w_in.shape), _const_spec(conv_w.shape),
              _const_spec(conv_b.shape), _const_spec(w_out.shape)]
    if g_final is not None:
        args.append(g_final)
        specs.append(_const_spec((1, D)))
    return pl.pallas_call(
        functools.partial(_ffn_kernel, tile=T, n_chunk=n_chunk, has_proj=proj is not None,
                          final_norm=g_final is not None),
        grid=(B, S // T),
        in_specs=specs,
        out_specs=pl.BlockSpec((None, T, D), lambda b, i: (b, i, 0)),
        out_shape=jax.ShapeDtypeStruct((B, S, D), F32),
        scratch_shapes=[pltpu.VMEM((T, n_chunk * FFN_CHUNK), BF16)],
        compiler_params=_params(2),
        name="conv_ffn",
    )(*args)


def _rope_tables(S):
    half = DIFF_HEAD_DIM // 2
    pos = jnp.arange(S, dtype=F32)
    inv_freq = ROPE_THETA ** (-jnp.arange(0, DIFF_HEAD_DIM, 2, dtype=F32) / DIFF_HEAD_DIM)
    ang = pos[:, None] * inv_freq[None, :]
    cos, sin = jnp.cos(ang), jnp.sin(ang)
    cq = jnp.concatenate([cos, cos], axis=1).T
    sq = jnp.concatenate([sin, sin], axis=1).T
    ck = jnp.concatenate([cos] * 4, axis=1)
    sk = jnp.concatenate([-sin, -sin, sin, sin], axis=1)
    return cq, sq, ck, sk


def _head_feature_order():
    half = DIFF_HEAD_DIM // 2
    n = jnp.arange(DIFF_V_DIM)
    part, c, i = n // DIFF_HEAD_DIM, (n % DIFF_HEAD_DIM) // half, n % half
    within = c * DIFF_HEAD_DIM + part * half + i
    return (jnp.arange(DIFF_HEADS)[:, None] * DIFF_V_DIM + within[None, :]).reshape(-1)


def _prep_ffn(w_in, conv_w, conv_b, w_out):
    D, F2 = w_in.shape
    F = F2 // 2
    assert F % FFN_CHUNK == 0
    n = F // FFN_CHUNK

    def pair(a):
        g = a[..., :F].reshape(a.shape[:-1] + (n, FFN_CHUNK))
        u = a[..., F:].reshape(a.shape[:-1] + (n, FFN_CHUNK))
        return jnp.moveaxis(jnp.concatenate([g, u], axis=-1), -2, 0)

    return (pair(w_in).astype(BF16), pair(conv_w), pair(conv_b[None, :]),
            w_out.astype(BF16))


def _trunk(x, p):
    B, S, D = x.shape
    depth = p["norm_mix"].shape[0]
    T = min(ROW_TILE, S)
    tables = _rope_tables(S)
    pending = None
    for i in range(depth):
        j = i // 2
        g_mix = p["norm_mix"][i][None, :]
        if i % 2 == 0:
            x = _gmlp_layer(x, g_mix, *p["gmlp"][j])
        else:
            wqv_t, wk, w_o, lam, subln_g = p["diff"][j]
            qa, qb, k, vt = _qkv_rope(x, g_mix, wqv_t, wk, tables, min(K_TILE, S))
            lam_init = 0.8 - 0.6 * math.exp(-0.3 * i)
            o = _diff_attn(qa, qb, k, vt, *lam, subln_g, lam_init, min(Q_TILE, S))
            pending = (o, w_o)
        g_final = p["norm_final"][None, :] if i == depth - 1 else None
        x = _conv_ffn(x, p["norm_ffn"][i][None, :], *p["ffn"][i], proj=pending, g_final=g_final)
        pending = None
    return x


def kernel(x_prompt, x_sample, norm_mix, norm_ffn, norm_final, gmlp_w_in, gmlp_v_gain, gmlp_w_s, gmlp_b_s, gmlp_w_out, diff_w_qkv, diff_lam_q1, diff_lam_k1, diff_lam_q2, diff_lam_k2, diff_subln_g, diff_w_out, ffn_w_in, ffn_conv_w, ffn_conv_b, ffn_w_out):
    D = x_prompt.shape[-1]
    order = _head_feature_order()
    gmlp = []
    for j in range(gmlp_w_in.shape[0]):
        b_full = jnp.repeat(gmlp_b_s[j].T, CHUNK, axis=1)
        gmlp.append((gmlp_w_in[j].astype(BF16), gmlp_v_gain[j][None, :], gmlp_w_s[j].astype(BF16),
                     b_full, gmlp_w_out[j].astype(BF16)))
    diff = []
    for j in range(diff_w_qkv.shape[0]):
        w = diff_w_qkv[j]
        wq, wk, wv = w[:, :D][:, order], w[:, D:2 * D][:, order], w[:, 2 * D:]
        wqv_t = jnp.concatenate([wq, wv], axis=1).T.astype(BF16)
        lam = tuple(a[j][None, :] for a in (diff_lam_q1, diff_lam_k1, diff_lam_q2, diff_lam_k2))
        diff.append((wqv_t, wk.astype(BF16), diff_w_out[j].astype(BF16), lam,
                     diff_subln_g[j][:, None]))
    ffn = [_prep_ffn(ffn_w_in[i], ffn_conv_w[i], ffn_conv_b[i], ffn_w_out[i])
           for i in range(ffn_w_in.shape[0])]
    p = dict(norm_mix=norm_mix, norm_ffn=norm_ffn, norm_final=norm_final,
             gmlp=gmlp, diff=diff, ffn=ffn)
    return (_trunk(x_prompt, p), _trunk(x_sample, p))
```

```python
import functools
import math

import jax
import jax.numpy as jnp
from jax import lax
from jax.experimental import pallas as pl
from jax.experimental.pallas import tpu as pltpu

CHUNK = 128
GMLP_GROUPS = 8
DIFF_HEADS = 8
DIFF_HEAD_DIM = 64
DIFF_V_DIM = 2 * DIFF_HEAD_DIM
CONV_WIDTH = 3
ROPE_THETA = 10000.0
NORM_EPS = 1e-6
SUBLN_EPS = 1e-5

LANES = 128
SUBLANES = 8
MXU_WIDTH = 256
ROW_TILE = 512
Q_TILE = 1024
K_TILE = 512
SOFTMAX_ROWS = 64
SUM_ROWS = 16
FFN_CHUNK = MXU_WIDTH
VMEM_LIMIT = 56 * 1024 * 1024

F32 = jnp.float32
BF16 = jnp.bfloat16


def _dot(a, b):
    return jnp.dot(a, b, preferred_element_type=F32)


def _rms(x, g, eps):
    ms = jnp.mean(x * x, axis=-1, keepdims=True)
    return x * lax.rsqrt(ms + eps) * g


def _const_spec(shape):
    zeros = (0,) * len(shape)
    return pl.BlockSpec(shape, lambda *_: zeros, pipeline_mode=pl.Buffered(1))


def _params(n_grid, flags=None):
    return pltpu.CompilerParams(
        dimension_semantics=("arbitrary",) * n_grid, vmem_limit_bytes=VMEM_LIMIT, flags=flags)


def _gmlp_kernel(x_ref, g_ref, win_ref, vg_ref, ws_ref, bs_ref, wout_ref, o_ref, y_ref,
                 *, n_chunks, width):
    x = x_ref[...]
    hb = _rms(x, g_ref[...], NORM_EPS).astype(BF16)
    z = _dot(hb, win_ref[...])
    z = 0.5 * z * (1.0 + jnp.tanh(math.sqrt(2.0 / math.pi) * (z + 0.044715 * (z * z * z))))
    u = z[:, :width]
    v = _rms(z[:, width:], vg_ref[...], NORM_EPS).astype(BF16)
    for g in range(GMLP_GROUPS):
        cols = slice(g * CHUNK, (g + 1) * CHUNK)
        vcat = jnp.concatenate(
            [v[c * CHUNK:(c + 1) * CHUNK, cols] for c in range(n_chunks)], axis=1)
        r = _dot(ws_ref[g], vcat)
        b = bs_ref[:, cols]
        for c in range(n_chunks):
            rows = slice(c * CHUNK, (c + 1) * CHUNK)
            y_ref[rows, cols] = (u[rows, cols] * (r[:, c * CHUNK:(c + 1) * CHUNK] + b)).astype(BF16)
    o_ref[...] = x + _dot(y_ref[...], wout_ref[...])


def _gmlp_layer(x, g_mix, w_in, v_gain, w_s, b_full, w_out):
    B, S, D = x.shape
    W = w_out.shape[0]
    T = min(ROW_TILE, S)
    assert S % T == 0 and T % CHUNK == 0
    row = pl.BlockSpec((None, T, D), lambda b, i: (b, i, 0))
    return pl.pallas_call(
        functools.partial(_gmlp_kernel, n_chunks=T // CHUNK, width=W),
        grid=(B, S // T),
        in_specs=[row, _const_spec((1, D)), _const_spec((D, 2 * W)), _const_spec((1, W)),
                  _const_spec((GMLP_GROUPS, CHUNK, CHUNK)), _const_spec((CHUNK, W)),
                  _const_spec((W, D))],
        out_specs=row,
        out_shape=jax.ShapeDtypeStruct((B, S, D), F32),
        scratch_shapes=[pltpu.VMEM((T, W), BF16)],
        compiler_params=_params(2),
        name="gmlp_layer",
    )(x, g_mix, w_in, v_gain, w_s, b_full, w_out)


def _qkv_kernel(x_ref, g_ref, wqv_ref, wk_ref, cq_ref, sq_ref, ck_ref, sk_ref,
                qa_ref, qb_ref, k_ref, vt_ref, *, d_model):
    hb = _rms(x_ref[...], g_ref[...], NORM_EPS).astype(BF16)
    qv_t = lax.dot_general(wqv_ref[...], hb, (((1,), (1,)), ((), ())),
                           preferred_element_type=F32)
    kk = _dot(hb, wk_ref[...])
    cq, sq = cq_ref[...], sq_ref[...]
    ck, sk = ck_ref[...], sk_ref[...]
    scale = DIFF_HEAD_DIM ** -0.5 * math.log2(math.e)
    half = DIFF_HEAD_DIM // 2
    zero = jnp.zeros((half, hb.shape[0]), F32)
    ones = jnp.ones((SUM_ROWS, hb.shape[0]), F32)
    for h in range(DIFF_HEADS):
        xh = qv_t[h * DIFF_V_DIM:(h + 1) * DIFF_V_DIM]
        x1, x2 = xh[:DIFF_HEAD_DIM], xh[DIFF_HEAD_DIM:]
        o1 = (x1 * cq - x2 * sq) * scale
        o2 = (x2 * cq + x1 * sq) * scale
        qa_ref[h] = jnp.concatenate([o1[:half], zero, o2[:half], zero], axis=0).astype(BF16)
        qb_ref[h] = jnp.concatenate([zero, o1[half:], zero, o2[half:]], axis=0).astype(BF16)
        vt_ref[h, 0] = jnp.concatenate(
            [qv_t[d_model + h * DIFF_V_DIM:d_model + (h + 1) * DIFF_V_DIM], ones], axis=0).astype(BF16)
        kh = kk[:, h * DIFF_V_DIM:(h + 1) * DIFF_V_DIM]
        k_ref[h] = (kh * ck + pltpu.roll(kh, DIFF_HEAD_DIM, 1) * sk).astype(BF16)


def _qkv_rope(x, g_mix, wqv_t, wk, tables, T):
    B, S, D = x.shape
    H = DIFF_HEADS
    cq, sq, ck, sk = tables
    hd = DIFF_V_DIM
    q_spec = pl.BlockSpec((None, H, hd, T), lambda b, i: (b, 0, 0, i))
    return pl.pallas_call(
        functools.partial(_qkv_kernel, d_model=D),
        grid=(B, S // T),
        in_specs=[pl.BlockSpec((None, T, D), lambda b, i: (b, i, 0)),
                  _const_spec((1, D)), _const_spec((2 * D, D)), _const_spec((D, D)),
                  pl.BlockSpec((DIFF_HEAD_DIM, T), lambda b, i: (0, i)),
                  pl.BlockSpec((DIFF_HEAD_DIM, T), lambda b, i: (0, i)),
                  pl.BlockSpec((T, hd), lambda b, i: (i, 0)),
                  pl.BlockSpec((T, hd), lambda b, i: (i, 0))],
        out_specs=[q_spec, q_spec,
                   pl.BlockSpec((None, H, T, hd), lambda b, i: (b, 0, i, 0)),
                   pl.BlockSpec((None, H, 1, hd + SUM_ROWS, T), lambda b, i: (b, 0, i, 0, 0))],
        out_shape=[jax.ShapeDtypeStruct((B, H, hd, S), BF16),
                   jax.ShapeDtypeStruct((B, H, hd, S), BF16),
                   jax.ShapeDtypeStruct((B, H, S, hd), BF16),
                   jax.ShapeDtypeStruct((B, H, S // T, hd + SUM_ROWS, T), BF16)],
        compiler_params=_params(2),
        name="qkv_rope",
    )(x, g_mix, wqv_t, wk, cq, sq, ck, sk)


def _attn_kernel(qa_ref, qb_ref, k_ref, vt_ref, lq1_ref, lk1_ref, lq2_ref, lk2_ref, g_ref,
                 o_ref, sa_ref, sb_ref, pa_ref, pb_ref, acca_ref, accb_ref, *, tk, lam_init):
    qa, qb = qa_ref[...], qb_ref[...]
    tq = qa.shape[1]
    n_k = k_ref.shape[0] // tk

    def k_block(i):
        return k_ref[pl.ds(pl.multiple_of(i * tk, tk), tk), :]

    def col_max(s):
        part = jnp.max(s.reshape(s.shape[0] // SUBLANES, SUBLANES, s.shape[1]), axis=0)
        return jnp.max(part, axis=0, keepdims=True)

    def scores(i, q, s_ref):
        s = _dot(k_block(i), q)
        s_ref[...] = s
        return col_max(s)

    def mxu_step(k_idx, q, s_ref, v_idx, p_ref, acc_ref, alpha):
        k, vt = k_block(k_idx), vt_ref[v_idx]
        mx = []
        for c in range(0, tq, MXU_WIDTH):
            cols = slice(c, c + MXU_WIDTH)
            s = _dot(k, q[:, cols])
            s_ref[:, cols] = s
            mx.append(col_max(s))
            acc_ref[:, cols] = acc_ref[:, cols] * alpha[:, cols] + _dot(vt, p_ref[:, cols])
        return jnp.concatenate(mx, axis=1)

    def softmax_step(s_ref, p_ref, m_blk, m):
        m_new = jnp.maximum(m, m_blk)
        for r in range(0, tk, SOFTMAX_ROWS):
            p_ref[r:r + SOFTMAX_ROWS, :] = jnp.exp2(s_ref[r:r + SOFTMAX_ROWS, :] - m_new).astype(BF16)
        return m_new, jnp.exp2(m - m_new)

    def half_step(j, u, carry):
        mxa, ma, mb, alpha_b = carry
        w = 1 - u
        mxb = mxu_step(j, qb, sb_ref.at[u], jnp.maximum(j - 1, 0), pb_ref.at[w], accb_ref, alpha_b)
        ma, alpha_a = softmax_step(sa_ref.at[u], pa_ref.at[u], mxa, ma)
        mxa = mxu_step(jnp.minimum(j + 1, n_k - 1), qa, sa_ref.at[w], j, pa_ref.at[u], acca_ref, alpha_a)
        mb, alpha_b = softmax_step(sb_ref.at[u], pb_ref.at[u], mxb, mb)
        return mxa, ma, mb, alpha_b

    mxa0 = scores(0, qa, sa_ref.at[0])
    pb_ref[1] = jnp.zeros((tk, tq), BF16)
    acca_ref[...] = jnp.zeros_like(acca_ref)
    accb_ref[...] = jnp.zeros_like(accb_ref)
    neg = jnp.full((1, tq), -jnp.inf, F32)
    _, _, _, alpha_b = lax.fori_loop(
        0, n_k // 2, lambda i, c: half_step(2 * i + 1, 1, half_step(2 * i, 0, c)),
        (mxa0, neg, neg, jnp.ones((1, tq), F32)))
    acc_a = acca_ref[...]
    acc_b = accb_ref[...] * alpha_b + _dot(vt_ref[n_k - 1], pb_ref[1])
    hd = DIFF_V_DIM
    la, lb = acc_a[hd:hd + 1], acc_b[hd:hd + 1]
    lam = (jnp.exp(jnp.sum(lq1_ref[...] * lk1_ref[...], axis=1, keepdims=True))
           - jnp.exp(jnp.sum(lq2_ref[...] * lk2_ref[...], axis=1, keepdims=True)) + lam_init)
    o = acc_a[:hd] / la - lam * (acc_b[:hd] / lb)
    ms = jnp.mean(o * o, axis=0, keepdims=True)
    o = o * lax.rsqrt(ms + SUBLN_EPS) * g_ref[...] * (1.0 - lam_init)
    o_ref[...] = o.T.astype(BF16)


def _diff_attn(qa, qb, k, vt, lam_q1, lam_k1, lam_q2, lam_k2, subln_g, lam_init, tq):
    B, H, hd, S = qa.shape
    n_kb, vt_rows, tk = vt.shape[2:]
    assert n_kb % 2 == 0 and S % tq == 0
    q_spec = pl.BlockSpec((None, None, hd, tq), lambda b, h, i: (b, h, 0, i))
    lam_spec = _const_spec((1, DIFF_HEAD_DIM))
    return pl.pallas_call(
        functools.partial(_attn_kernel, tk=tk, lam_init=lam_init),
        grid=(B, H, S // tq),
        in_specs=[q_spec, q_spec,
                  pl.BlockSpec((None, None, S, hd), lambda b, h, i: (b, h, 0, 0)),
                  pl.BlockSpec((None, None, n_kb, vt_rows, tk), lambda b, h, i: (b, h, 0, 0, 0)),
                  lam_spec, lam_spec, lam_spec, lam_spec, _const_spec((hd, 1))],
        out_specs=pl.BlockSpec((None, tq, hd), lambda b, h, i: (b, i, h)),
        out_shape=jax.ShapeDtypeStruct((B, S, H * hd), BF16),
        scratch_shapes=[pltpu.VMEM((2, tk, tq), F32), pltpu.VMEM((2, tk, tq), F32),
                        pltpu.VMEM((2, tk, tq), BF16), pltpu.VMEM((2, tk, tq), BF16),
                        pltpu.VMEM((vt_rows, tq), F32), pltpu.VMEM((vt_rows, tq), F32)],
        compiler_params=_params(3),
        name="diff_attn",
    )(qa, qb, k, vt, lam_q1, lam_k1, lam_q2, lam_k2, subln_g)


def _ffn_kernel(*refs, tile, n_chunk, has_proj, final_norm):
    refs = list(refs)
    x_ref, xp_ref, xn_ref = refs[:3]
    del refs[:3]
    if has_proj:
        o_ref, op_ref, on_ref, wo_ref = refs[:4]
        del refs[:4]
    gn_ref, win_ref, cw_ref, cb_ref, wout_ref = refs[:5]
    del refs[:5]
    if final_norm:
        gf_ref = refs.pop(0)
    out_ref, y_ref = refs

    halo = SUBLANES
    rows = tile + 2 * halo
    i, n = pl.program_id(1), pl.num_programs(1)
    xa = jnp.concatenate([xp_ref[...], x_ref[...], xn_ref[...]], axis=0)
    if has_proj:
        oa = jnp.concatenate([op_ref[...], o_ref[...], on_ref[...]], axis=0)
        xa = xa + _dot(oa, wo_ref[...])
    h = _rms(xa, gn_ref[...], NORM_EPS)
    r = lax.broadcasted_iota(jnp.int32, (rows, 1), 0)
    inside = ((r >= halo) | (i > 0)) & ((r < tile + halo) | (i < n - 1))
    hb = jnp.where(inside, h, 0.0).astype(BF16)
    n_vr = tile // SUBLANES
    sub = lax.broadcasted_iota(jnp.int32, (1, SUBLANES, 1), 1)
    for j in range(n_chunk):
        a = _dot(hb, win_ref[j])
        cw = cw_ref[j]
        a3 = a.reshape(n_vr + 2, SUBLANES, 2 * FFN_CHUNK)
        dn = pltpu.roll(a3, 1, 1)
        up = pltpu.roll(a3, SUBLANES - 1, 1)
        a_prev = jnp.where(sub == 0, dn[0:n_vr], dn[1:n_vr + 1])
        a_next = jnp.where(sub == SUBLANES - 1, up[2:n_vr + 2], up[1:n_vr + 1])
        c = cb_ref[j] + a_prev * cw[0:1] + a3[1:n_vr + 1] * cw[1:2] + a_next * cw[2:3]
        c = c.reshape(tile, 2 * FFN_CHUNK)
        gate, lin = c[:, :FFN_CHUNK], c[:, FFN_CHUNK:]
        y_ref[:, j * FFN_CHUNK:(j + 1) * FFN_CHUNK] = (
            gate * (1.0 / (1.0 + jnp.exp(-gate))) * lin).astype(BF16)
    out = xa[halo:halo + tile] + _dot(y_ref[...], wout_ref[...])
    if final_norm:
        out = _rms(out, gf_ref[...], NORM_EPS)
    out_ref[...] = out


def _conv_ffn(x, g_ffn, w_in, conv_w, conv_b, w_out, proj=None, g_final=None):
    B, S, D = x.shape
    T = min(ROW_TILE, S)
    halo = SUBLANES
    n_chunk = w_in.shape[0]
    assert S % T == 0 and T % halo == 0
    tb, last = T // halo, S // halo - 1

    def triple():
        return [pl.BlockSpec((None, T, D), lambda b, i: (b, i, 0)),
                pl.BlockSpec((None, halo, D), lambda b, i: (b, jnp.maximum(i * tb - 1, 0), 0)),
                pl.BlockSpec((None, halo, D), lambda b, i: (b, jnp.minimum((i + 1) * tb, last), 0))]

    args, specs = [x, x, x], triple()
    if proj is not None:
        o, w_o = proj
        args += [o, o, o, w_o]
        specs += triple() + [_const_spec(w_o.shape)]
    args += [g_ffn, w_in, conv_w, conv_b, w_out]
    specs += [_const_spec((1, D)), _const_spec(w_in.shape), _const_spec(conv_w.shape),
              _const_spec(conv_b.shape), _const_spec(w_out.shape)]
    if g_final is not None:
        args.append(g_final)
        specs.append(_const_spec((1, D)))
    return pl.pallas_call(
        functools.partial(_ffn_kernel, tile=T, n_chunk=n_chunk, has_proj=proj is not None,
                          final_norm=g_final is not None),
        grid=(B, S // T),
        in_specs=specs,
        out_specs=pl.BlockSpec((None, T, D), lambda b, i: (b, i, 0)),
        out_shape=jax.ShapeDtypeStruct((B, S, D), F32),
        scratch_shapes=[pltpu.VMEM((T, n_chunk * FFN_CHUNK), BF16)],
        compiler_params=_params(2),
        name="conv_ffn",
    )(*args)


def _rope_tables(S):
    half = DIFF_HEAD_DIM // 2
    pos = jnp.arange(S, dtype=F32)
    inv_freq = ROPE_THETA ** (-jnp.arange(0, DIFF_HEAD_DIM, 2, dtype=F32) / DIFF_HEAD_DIM)
    ang = pos[:, None] * inv_freq[None, :]
    cos, sin = jnp.cos(ang), jnp.sin(ang)
    cq = jnp.concatenate([cos, cos], axis=1).T
    sq = jnp.concatenate([sin, sin], axis=1).T
    ck = jnp.concatenate([cos] * 4, axis=1)
    sk = jnp.concatenate([-sin, -sin, sin, sin], axis=1)
    return cq, sq, ck, sk


def _head_feature_order():
    half = DIFF_HEAD_DIM // 2
    n = jnp.arange(DIFF_V_DIM)
    part, c, i = n // DIFF_HEAD_DIM, (n % DIFF_HEAD_DIM) // half, n % half
    within = c * DIFF_HEAD_DIM + part * half + i
    return (jnp.arange(DIFF_HEADS)[:, None] * DIFF_V_DIM + within[None, :]).reshape(-1)


def _prep_ffn(w_in, conv_w, conv_b, w_out):
    D, F2 = w_in.shape
    F = F2 // 2
    assert F % FFN_CHUNK == 0
    n = F // FFN_CHUNK

    def pair(a):
        g = a[..., :F].reshape(a.shape[:-1] + (n, FFN_CHUNK))
        u = a[..., F:].reshape(a.shape[:-1] + (n, FFN_CHUNK))
        return jnp.moveaxis(jnp.concatenate([g, u], axis=-1), -2, 0)

    return (pair(w_in).astype(BF16), pair(conv_w), pair(conv_b[None, :]),
            w_out.astype(BF16))


def _trunk(x, p):
    B, S, D = x.shape
    depth = p["norm_mix"].shape[0]
    tables = _rope_tables(S)
    pending = None
    for i in range(depth):
        j = i // 2
        g_mix = p["norm_mix"][i][None, :]
        if i % 2 == 0:
            x = _gmlp_layer(x, g_mix, *p["gmlp"][j])
        else:
            wqv_t, wk, w_o, lam, subln_g = p["diff"][j]
            qa, qb, k, vt = _qkv_rope(x, g_mix, wqv_t, wk, tables, min(K_TILE, S))
            lam_init = 0.8 - 0.6 * math.exp(-0.3 * i)
            o = _diff_attn(qa, qb, k, vt, *lam, subln_g, lam_init, min(Q_TILE, S))
            pending = (o, w_o)
        g_final = p["norm_final"][None, :] if i == depth - 1 else None
        x = _conv_ffn(x, p["norm_ffn"][i][None, :], *p["ffn"][i], proj=pending, g_final=g_final)
        pending = None
    return x


def kernel(x_prompt, x_sample, norm_mix, norm_ffn, norm_final, gmlp_w_in, gmlp_v_gain, gmlp_w_s, gmlp_b_s, gmlp_w_out, diff_w_qkv, diff_lam_q1, diff_lam_k1, diff_lam_q2, diff_lam_k2, diff_subln_g, diff_w_out, ffn_w_in, ffn_conv_w, ffn_conv_b, ffn_w_out):
    D = x_prompt.shape[-1]
    order = _head_feature_order()
    gmlp = []
    for j in range(gmlp_w_in.shape[0]):
        b_full = jnp.repeat(gmlp_b_s[j].T, CHUNK, axis=1)
        gmlp.append((gmlp_w_in[j].astype(BF16), gmlp_v_gain[j][None, :], gmlp_w_s[j].astype(BF16),
                     b_full, gmlp_w_out[j].astype(BF16)))
    diff = []
    for j in range(diff_w_qkv.shape[0]):
        w = diff_w_qkv[j]
        wq, wk, wv = w[:, :D][:, order], w[:, D:2 * D][:, order], w[:, 2 * D:]
        wqv_t = jnp.concatenate([wq, wv], axis=1).T.astype(BF16)
        lam = tuple(a[j][None, :] for a in (diff_lam_q1, diff_lam_k1, diff_lam_q2, diff_lam_k2))
        diff.append((wqv_t, wk.astype(BF16), diff_w_out[j].astype(BF16), lam,
                     diff_subln_g[j][:, None]))
    ffn = [_prep_ffn(ffn_w_in[i], ffn_conv_w[i], ffn_conv_b[i], ffn_w_out[i])
           for i in range(ffn_w_in.shape[0])]
    p = dict(norm_mix=norm_mix, norm_ffn=norm_ffn, norm_final=norm_final,
             gmlp=gmlp, diff=diff, ffn=ffn)
    return (_trunk(x_prompt, p), _trunk(x_sample, p))
```

```python
import functools
import math

import jax
import jax.numpy as jnp
from jax import lax
from jax.experimental import pallas as pl
from jax.experimental.pallas import tpu as pltpu

CHUNK = 128
GMLP_GROUPS = 8
DIFF_HEADS = 8
DIFF_HEAD_DIM = 64
DIFF_V_DIM = 2 * DIFF_HEAD_DIM
CONV_WIDTH = 3
ROPE_THETA = 10000.0
NORM_EPS = 1e-6
SUBLN_EPS = 1e-5

LANES = 128
SUBLANES = 8
MXU_WIDTH = 256
ROW_TILE = 512
Q_TILE = 1024
K_TILE = 512
SOFTMAX_ROWS = 64
SUM_ROWS = 16
FFN_CHUNK = MXU_WIDTH
VMEM_LIMIT = 56 * 1024 * 1024

F32 = jnp.float32
BF16 = jnp.bfloat16


def _dot(a, b):
    return jnp.dot(a, b, preferred_element_type=F32)


def _rms(x, g, eps):
    ms = jnp.mean(x * x, axis=-1, keepdims=True)
    return x * lax.rsqrt(ms + eps) * g


def _const_spec(shape):
    zeros = (0,) * len(shape)
    return pl.BlockSpec(shape, lambda *_: zeros, pipeline_mode=pl.Buffered(1))


def _params(n_grid, flags=None):
    return pltpu.CompilerParams(
        dimension_semantics=("arbitrary",) * n_grid, vmem_limit_bytes=VMEM_LIMIT, flags=flags)


def _gmlp_kernel(x_ref, g_ref, win_ref, vg_ref, ws_ref, bs_ref, wout_ref, o_ref, y_ref,
                 *, n_chunks, width):
    x = x_ref[...]
    hb = _rms(x, g_ref[...], NORM_EPS).astype(BF16)
    z = _dot(hb, win_ref[...])
    z = 0.5 * z * (1.0 + jnp.tanh(math.sqrt(2.0 / math.pi) * (z + 0.044715 * (z * z * z))))
    u = z[:, :width]
    v = _rms(z[:, width:], vg_ref[...], NORM_EPS).astype(BF16)
    for g in range(GMLP_GROUPS):
        cols = slice(g * CHUNK, (g + 1) * CHUNK)
        vcat = jnp.concatenate(
            [v[c * CHUNK:(c + 1) * CHUNK, cols] for c in range(n_chunks)], axis=1)
        r = _dot(ws_ref[g], vcat)
        b = bs_ref[:, cols]
        for c in range(n_chunks):
            rows = slice(c * CHUNK, (c + 1) * CHUNK)
            y_ref[rows, cols] = (u[rows, cols] * (r[:, c * CHUNK:(c + 1) * CHUNK] + b)).astype(BF16)
    o_ref[...] = x + _dot(y_ref[...], wout_ref[...])


def _gmlp_layer(x, g_mix, w_in, v_gain, w_s, b_full, w_out):
    B, S, D = x.shape
    W = w_out.shape[0]
    T = min(ROW_TILE, S)
    assert S % T == 0 and T % CHUNK == 0
    row = pl.BlockSpec((None, T, D), lambda b, i: (b, i, 0))
    return pl.pallas_call(
        functools.partial(_gmlp_kernel, n_chunks=T // CHUNK, width=W),
        grid=(B, S // T),
        in_specs=[row, _const_spec((1, D)), _const_spec((D, 2 * W)), _const_spec((1, W)),
                  _const_spec((GMLP_GROUPS, CHUNK, CHUNK)), _const_spec((CHUNK, W)),
                  _const_spec((W, D))],
        out_specs=row,
        out_shape=jax.ShapeDtypeStruct((B, S, D), F32),
        scratch_shapes=[pltpu.VMEM((T, W), BF16)],
        compiler_params=_params(2),
        name="gmlp_layer",
    )(x, g_mix, w_in, v_gain, w_s, b_full, w_out)


def _qkv_kernel(x_ref, g_ref, wqv_ref, wk_ref, cq_ref, sq_ref, ck_ref, sk_ref,
                qa_ref, qb_ref, k_ref, vt_ref, *, d_model):
    hb = _rms(x_ref[...], g_ref[...], NORM_EPS).astype(BF16)
    qv_t = lax.dot_general(wqv_ref[...], hb, (((1,), (1,)), ((), ())),
                           preferred_element_type=F32)
    kk = _dot(hb, wk_ref[...])
    cq, sq = cq_ref[...], sq_ref[...]
    ck, sk = ck_ref[...], sk_ref[...]
    scale = DIFF_HEAD_DIM ** -0.5 * math.log2(math.e)
    half = DIFF_HEAD_DIM // 2
    zero = jnp.zeros((half, hb.shape[0]), F32)
    ones = jnp.ones((SUM_ROWS, hb.shape[0]), F32)
    for h in range(DIFF_HEADS):
        xh = qv_t[h * DIFF_V_DIM:(h + 1) * DIFF_V_DIM]
        x1, x2 = xh[:DIFF_HEAD_DIM], xh[DIFF_HEAD_DIM:]
        o1 = (x1 * cq - x2 * sq) * scale
        o2 = (x2 * cq + x1 * sq) * scale
        qa_ref[h] = jnp.concatenate([o1[:half], zero, o2[:half], zero], axis=0).astype(BF16)
        qb_ref[h] = jnp.concatenate([zero, o1[half:], zero, o2[half:]], axis=0).astype(BF16)
        vt_ref[h, 0] = jnp.concatenate(
            [qv_t[d_model + h * DIFF_V_DIM:d_model + (h + 1) * DIFF_V_DIM], ones], axis=0).astype(BF16)
        kh = kk[:, h * DIFF_V_DIM:(h + 1) * DIFF_V_DIM]
        k_ref[h] = (kh * ck + pltpu.roll(kh, DIFF_HEAD_DIM, 1) * sk).astype(BF16)


def _qkv_rope(x, g_mix, wqv_t, wk, tables, T):
    B, S, D = x.shape
    H = DIFF_HEADS
    cq, sq, ck, sk = tables
    hd = DIFF_V_DIM
    q_spec = pl.BlockSpec((None, H, hd, T), lambda b, i: (b, 0, 0, i))
    return pl.pallas_call(
        functools.partial(_qkv_kernel, d_model=D),
        grid=(B, S // T),
        in_specs=[pl.BlockSpec((None, T, D), lambda b, i: (b, i, 0)),
                  _const_spec((1, D)), _const_spec((2 * D, D)), _const_spec((D, D)),
                  pl.BlockSpec((DIFF_HEAD_DIM, T), lambda b, i: (0, i)),
                  pl.BlockSpec((DIFF_HEAD_DIM, T), lambda b, i: (0, i)),
                  pl.BlockSpec((T, hd), lambda b, i: (i, 0)),
                  pl.BlockSpec((T, hd), lambda b, i: (i, 0))],
        out_specs=[q_spec, q_spec,
                   pl.BlockSpec((None, H, T, hd), lambda b, i: (b, 0, i, 0)),
                   pl.BlockSpec((None, H, 1, hd + SUM_ROWS, T), lambda b, i: (b, 0, i, 0, 0))],
        out_shape=[jax.ShapeDtypeStruct((B, H, hd, S), BF16),
                   jax.ShapeDtypeStruct((B, H, hd, S), BF16),
                   jax.ShapeDtypeStruct((B, H, S, hd), BF16),
                   jax.ShapeDtypeStruct((B, H, S // T, hd + SUM_ROWS, T), BF16)],
        compiler_params=_params(2),
        name="qkv_rope",
    )(x, g_mix, wqv_t, wk, cq, sq, ck, sk)


def _attn_kernel(qa_ref, qb_ref, k_ref, vt_ref, lq1_ref, lk1_ref, lq2_ref, lk2_ref, g_ref,
                 o_ref, sa_ref, sb_ref, pa_ref, pb_ref, acca_ref, accb_ref, *, tk, lam_init):
    qa, qb = qa_ref[...], qb_ref[...]
    tq = qa.shape[1]
    n_k = k_ref.shape[0] // tk

    def k_block(i):
        return k_ref[pl.ds(pl.multiple_of(i * tk, tk), tk), :]

    def col_max(s):
        part = jnp.max(s.reshape(s.shape[0] // SUBLANES, SUBLANES, s.shape[1]), axis=0)
        return jnp.max(part, axis=0, keepdims=True)

    def scores(i, q, s_ref):
        s = _dot(k_block(i), q)
        s_ref[...] = s
        return col_max(s)

    def mxu_step(k_idx, q, s_ref, v_idx, p_ref, acc_ref, alpha):
        k, vt = k_block(k_idx), vt_ref[v_idx]
        mx = []
        for c in range(0, tq, MXU_WIDTH):
            cols = slice(c, c + MXU_WIDTH)
            s = _dot(k, q[:, cols])
            s_ref[:, cols] = s
            mx.append(col_max(s))
            acc_ref[:, cols] = acc_ref[:, cols] * alpha[:, cols] + _dot(vt, p_ref[:, cols])
        return jnp.concatenate(mx, axis=1)

    def softmax_step(s_ref, p_ref, m_blk, m):
        m_new = jnp.maximum(m, m_blk)
        for r in range(0, tk, SOFTMAX_ROWS):
            p_ref[r:r + SOFTMAX_ROWS, :] = jnp.exp2(s_ref[r:r + SOFTMAX_ROWS, :] - m_new).astype(BF16)
        return m_new, jnp.exp2(m - m_new)

    def half_step(j, u, carry):
        mxa, ma, mb, alpha_b = carry
        w = 1 - u
        mxb = mxu_step(j, qb, sb_ref.at[u], jnp.maximum(j - 1, 0), pb_ref.at[w], accb_ref, alpha_b)
        ma, alpha_a = softmax_step(sa_ref.at[u], pa_ref.at[u], mxa, ma)
        mxa = mxu_step(jnp.minimum(j + 1, n_k - 1), qa, sa_ref.at[w], j, pa_ref.at[u], acca_ref, alpha_a)
        mb, alpha_b = softmax_step(sb_ref.at[u], pb_ref.at[u], mxb, mb)
        return mxa, ma, mb, alpha_b

    mxa0 = scores(0, qa, sa_ref.at[0])
    pb_ref[1] = jnp.zeros((tk, tq), BF16)
    acca_ref[...] = jnp.zeros_like(acca_ref)
    accb_ref[...] = jnp.zeros_like(accb_ref)
    neg = jnp.full((1, tq), -jnp.inf, F32)
    _, _, _, alpha_b = lax.fori_loop(
        0, n_k // 2, lambda i, c: half_step(2 * i + 1, 1, half_step(2 * i, 0, c)),
        (mxa0, neg, neg, jnp.ones((1, tq), F32)), unroll=2)
    acc_a = acca_ref[...]
    acc_b = accb_ref[...] * alpha_b + _dot(vt_ref[n_k - 1], pb_ref[1])
    hd = DIFF_V_DIM
    la, lb = acc_a[hd:hd + 1], acc_b[hd:hd + 1]
    lam = (jnp.exp(jnp.sum(lq1_ref[...] * lk1_ref[...], axis=1, keepdims=True))
           - jnp.exp(jnp.sum(lq2_ref[...] * lk2_ref[...], axis=1, keepdims=True)) + lam_init)
    o = acc_a[:hd] / la - lam * (acc_b[:hd] / lb)
    ms = jnp.mean(o * o, axis=0, keepdims=True)
    o = o * lax.rsqrt(ms + SUBLN_EPS) * g_ref[...] * (1.0 - lam_init)
    o_ref[...] = o.T.astype(BF16)


def _diff_attn(qa, qb, k, vt, lam_q1, lam_k1, lam_q2, lam_k2, subln_g, lam_init, tq):
    B, H, hd, S = qa.shape
    n_kb, vt_rows, tk = vt.shape[2:]
    assert n_kb % 2 == 0 and S % tq == 0
    q_spec = pl.BlockSpec((None, None, hd, tq), lambda b, h, i: (b, h, 0, i))
    lam_spec = _const_spec((1, DIFF_HEAD_DIM))
    return pl.pallas_call(
        functools.partial(_attn_kernel, tk=tk, lam_init=lam_init),
        grid=(B, H, S // tq),
        in_specs=[q_spec, q_spec,
                  pl.BlockSpec((None, None, S, hd), lambda b, h, i: (b, h, 0, 0)),
                  pl.BlockSpec((None, None, n_kb, vt_rows, tk), lambda b, h, i: (b, h, 0, 0, 0)),
                  lam_spec, lam_spec, lam_spec, lam_spec, _const_spec((hd, 1))],
        out_specs=pl.BlockSpec((None, tq, hd), lambda b, h, i: (b, i, h)),
        out_shape=jax.ShapeDtypeStruct((B, S, H * hd), BF16),
        scratch_shapes=[pltpu.VMEM((2, tk, tq), F32), pltpu.VMEM((2, tk, tq), F32),
                        pltpu.VMEM((2, tk, tq), BF16), pltpu.VMEM((2, tk, tq), BF16),
                        pltpu.VMEM((vt_rows, tq), F32), pltpu.VMEM((vt_rows, tq), F32)],
        compiler_params=_params(3),
        name="diff_attn",
    )(qa, qb, k, vt, lam_q1, lam_k1, lam_q2, lam_k2, subln_g)


def _ffn_kernel(*refs, tile, n_chunk, has_proj, final_norm):
    refs = list(refs)
    x_ref, xp_ref, xn_ref = refs[:3]
    del refs[:3]
    if has_proj:
        o_ref, op_ref, on_ref, wo_ref = refs[:4]
        del refs[:4]
    gn_ref, win_ref, cw_ref, cb_ref, wout_ref = refs[:5]
    del refs[:5]
    if final_norm:
        gf_ref = refs.pop(0)
    out_ref, y_ref = refs

    halo = SUBLANES
    rows = tile + 2 * halo
    i, n = pl.program_id(1), pl.num_programs(1)
    xa = jnp.concatenate([xp_ref[...], x_ref[...], xn_ref[...]], axis=0)
    if has_proj:
        oa = jnp.concatenate([op_ref[...], o_ref[...], on_ref[...]], axis=0)
        xa = xa + _dot(oa, wo_ref[...])
    h = _rms(xa, gn_ref[...], NORM_EPS)
    r = lax.broadcasted_iota(jnp.int32, (rows, 1), 0)
    inside = ((r >= halo) | (i > 0)) & ((r < tile + halo) | (i < n - 1))
    hb = jnp.where(inside, h, 0.0).astype(BF16)
    n_vr = tile // SUBLANES
    sub = lax.broadcasted_iota(jnp.int32, (1, SUBLANES, 1), 1)
    for j in range(n_chunk):
        a = _dot(hb, win_ref[j])
        cw = cw_ref[j]
        a3 = a.reshape(n_vr + 2, SUBLANES, 2 * FFN_CHUNK)
        dn = pltpu.roll(a3, 1, 1)
        up = pltpu.roll(a3, SUBLANES - 1, 1)
        a_prev = jnp.where(sub == 0, dn[0:n_vr], dn[1:n_vr + 1])
        a_next = jnp.where(sub == SUBLANES - 1, up[2:n_vr + 2], up[1:n_vr + 1])
        c = cb_ref[j] + a_prev * cw[0:1] + a3[1:n_vr + 1] * cw[1:2] + a_next * cw[2:3]
        c = c.reshape(tile, 2 * FFN_CHUNK)
        gate, lin = c[:, :FFN_CHUNK], c[:, FFN_CHUNK:]
        y_ref[:, j * FFN_CHUNK:(j + 1) * FFN_CHUNK] = (
            gate * (1.0 / (1.0 + jnp.exp(-gate))) * lin).astype(BF16)
    out = xa[halo:halo + tile] + _dot(y_ref[...], wout_ref[...])
    if final_norm:
        out = _rms(out, gf_ref[...], NORM_EPS)
    out_ref[...] = out


def _conv_ffn(x, g_ffn, w_in, conv_w, conv_b, w_out, proj=None, g_final=None):
    B, S, D = x.shape
    T = min(ROW_TILE, S)
    halo = SUBLANES
    n_chunk = w_in.shape[0]
    assert S % T == 0 and T % halo == 0
    tb, last = T // halo, S // halo - 1

    def triple():
        return [pl.BlockSpec((None, T, D), lambda b, i: (b, i, 0)),
                pl.BlockSpec((None, halo, D), lambda b, i: (b, jnp.maximum(i * tb - 1, 0), 0)),
                pl.BlockSpec((None, halo, D), lambda b, i: (b, jnp.minimum((i + 1) * tb, last), 0))]

    args, specs = [x, x, x], triple()
    if proj is not None:
        o, w_o = proj
        args += [o, o, o, w_o]
        specs += triple() + [_const_spec(w_o.shape)]
    args += [g_ffn, w_in, conv_w, conv_b, w_out]
    specs += [_const_spec((1, D)), _const_spec(w_in.shape), _const_spec(conv_w.shape),
              _const_spec(conv_b.shape), _const_spec(w_out.shape)]
    if g_final is not None:
        args.append(g_final)
        specs.append(_const_spec((1, D)))
    return pl.pallas_call(
        functools.partial(_ffn_kernel, tile=T, n_chunk=n_chunk, has_proj=proj is not None,
                          final_norm=g_final is not None),
        grid=(B, S // T),
        in_specs=specs,
        out_specs=pl.BlockSpec((None, T, D), lambda b, i: (b, i, 0)),
        out_shape=jax.ShapeDtypeStruct((B, S, D), F32),
        scratch_shapes=[pltpu.VMEM((T, n_chunk * FFN_CHUNK), BF16)],
        compiler_params=_params(2),
        name="conv_ffn",
    )(*args)


def _rope_tables(S):
    half = DIFF_HEAD_DIM // 2
    pos = jnp.arange(S, dtype=F32)
    inv_freq = ROPE_THETA ** (-jnp.arange(0, DIFF_HEAD_DIM, 2, dtype=F32) / DIFF_HEAD_DIM)
    ang = pos[:, None] * inv_freq[None, :]
    cos, sin = jnp.cos(ang), jnp.sin(ang)
    cq = jnp.concatenate([cos, cos], axis=1).T
    sq = jnp.concatenate([sin, sin], axis=1).T
    ck = jnp.concatenate([cos] * 4, axis=1)
    sk = jnp.concatenate([-sin, -sin, sin, sin], axis=1)
    return cq, sq, ck, sk


def _head_feature_order():
    half = DIFF_HEAD_DIM // 2
    n = jnp.arange(DIFF_V_DIM)
    part, c, i = n // DIFF_HEAD_DIM, (n % DIFF_HEAD_DIM) // half, n % half
    within = c * DIFF_HEAD_DIM + part * half + i
    return (jnp.arange(DIFF_HEADS)[:, None] * DIFF_V_DIM + within[None, :]).reshape(-1)


def _prep_ffn(w_in, conv_w, conv_b, w_out):
    D, F2 = w_in.shape
    F = F2 // 2
    assert F % FFN_CHUNK == 0
    n = F // FFN_CHUNK

    def pair(a):
        g = a[..., :F].reshape(a.shape[:-1] + (n, FFN_CHUNK))
        u = a[..., F:].reshape(a.shape[:-1] + (n, FFN_CHUNK))
        return jnp.moveaxis(jnp.concatenate([g, u], axis=-1), -2, 0)

    return (pair(w_in).astype(BF16), pair(conv_w), pair(conv_b[None, :]),
            w_out.astype(BF16))


def _trunk(x, p):
    B, S, D = x.shape
    depth = p["norm_mix"].shape[0]
    tables = _rope_tables(S)
    pending = None
    for i in range(depth):
        j = i // 2
        g_mix = p["norm_mix"][i][None, :]
        if i % 2 == 0:
            x = _gmlp_layer(x, g_mix, *p["gmlp"][j])
        else:
            wqv_t, wk, w_o, lam, subln_g = p["diff"][j]
            qa, qb, k, vt = _qkv_rope(x, g_mix, wqv_t, wk, tables, min(K_TILE, S))
            lam_init = 0.8 - 0.6 * math.exp(-0.3 * i)
            o = _diff_attn(qa, qb, k, vt, *lam, subln_g, lam_init, min(Q_TILE, S))
            pending = (o, w_o)
        g_final = p["norm_final"][None, :] if i == depth - 1 else None
        x = _conv_ffn(x, p["norm_ffn"][i][None, :], *p["ffn"][i], proj=pending, g_final=g_final)
        pending = None
    return x


def kernel(x_prompt, x_sample, norm_mix, norm_ffn, norm_final, gmlp_w_in, gmlp_v_gain, gmlp_w_s, gmlp_b_s, gmlp_w_out, diff_w_qkv, diff_lam_q1, diff_lam_k1, diff_lam_q2, diff_lam_k2, diff_subln_g, diff_w_out, ffn_w_in, ffn_conv_w, ffn_conv_b, ffn_w_out):
    D = x_prompt.shape[-1]
    order = _head_feature_order()
    gmlp = []
    for j in range(gmlp_w_in.shape[0]):
        b_full = jnp.repeat(gmlp_b_s[j].T, CHUNK, axis=1)
        gmlp.append((gmlp_w_in[j].astype(BF16), gmlp_v_gain[j][None, :], gmlp_w_s[j].astype(BF16),
                     b_full, gmlp_w_out[j].astype(BF16)))
    diff = []
    for j in range(diff_w_qkv.shape[0]):
        w = diff_w_qkv[j]
        wq, wk, wv = w[:, :D][:, order], w[:, D:2 * D][:, order], w[:, 2 * D:]
        wqv_t = jnp.concatenate([wq, wv], axis=1).T.astype(BF16)
        lam = tuple(a[j][None, :] for a in (diff_lam_q1, diff_lam_k1, diff_lam_q2, diff_lam_k2))
        diff.append((wqv_t, wk.astype(BF16), diff_w_out[j].astype(BF16), lam,
                     diff_subln_g[j][:, None]))
    ffn = [_prep_ffn(ffn_w_in[i], ffn_conv_w[i], ffn_conv_b[i], ffn_w_out[i])
           for i in range(ffn_w_in.shape[0])]
    p = dict(norm_mix=norm_mix, norm_ffn=norm_ffn, norm_final=norm_final,
             gmlp=gmlp, diff=diff, ffn=ffn)
    return (_trunk(x_prompt, p), _trunk(x_sample, p))
```

```python
import functools
import math

import jax
import jax.numpy as jnp
from jax import lax
from jax.experimental import pallas as pl
from jax.experimental.pallas import tpu as pltpu

CHUNK = 128
GMLP_GROUPS = 8
DIFF_HEADS = 8
DIFF_HEAD_DIM = 64
DIFF_V_DIM = 2 * DIFF_HEAD_DIM
CONV_WIDTH = 3
ROPE_THETA = 10000.0
NORM_EPS = 1e-6
SUBLN_EPS = 1e-5

LANES = 128
SUBLANES = 8
MXU_WIDTH = 256
ROW_TILE = 512
Q_TILE = 2048
K_TILE = 512
SOFTMAX_ROWS = 64
SUM_ROWS = 16
FFN_CHUNK = MXU_WIDTH
VMEM_LIMIT = 56 * 1024 * 1024

F32 = jnp.float32
BF16 = jnp.bfloat16


def _dot(a, b):
    return jnp.dot(a, b, preferred_element_type=F32)


def _rms(x, g, eps):
    ms = jnp.mean(x * x, axis=-1, keepdims=True)
    return x * lax.rsqrt(ms + eps) * g


def _const_spec(shape):
    zeros = (0,) * len(shape)
    return pl.BlockSpec(shape, lambda *_: zeros, pipeline_mode=pl.Buffered(1))


def _params(n_grid, flags=None):
    return pltpu.CompilerParams(
        dimension_semantics=("arbitrary",) * n_grid, vmem_limit_bytes=VMEM_LIMIT, flags=flags)


def _gmlp_kernel(x_ref, g_ref, win_ref, vg_ref, ws_ref, bs_ref, wout_ref, o_ref, y_ref,
                 *, n_chunks, width):
    x = x_ref[...]
    hb = _rms(x, g_ref[...], NORM_EPS).astype(BF16)
    z = _dot(hb, win_ref[...])
    z = 0.5 * z * (1.0 + jnp.tanh(math.sqrt(2.0 / math.pi) * (z + 0.044715 * (z * z * z))))
    u = z[:, :width]
    v = _rms(z[:, width:], vg_ref[...], NORM_EPS).astype(BF16)
    for g in range(GMLP_GROUPS):
        cols = slice(g * CHUNK, (g + 1) * CHUNK)
        vcat = jnp.concatenate(
            [v[c * CHUNK:(c + 1) * CHUNK, cols] for c in range(n_chunks)], axis=1)
        r = _dot(ws_ref[g], vcat)
        b = bs_ref[:, cols]
        for c in range(n_chunks):
            rows = slice(c * CHUNK, (c + 1) * CHUNK)
            y_ref[rows, cols] = (u[rows, cols] * (r[:, c * CHUNK:(c + 1) * CHUNK] + b)).astype(BF16)
    o_ref[...] = x + _dot(y_ref[...], wout_ref[...])


def _gmlp_layer(x, g_mix, w_in, v_gain, w_s, b_full, w_out):
    B, S, D = x.shape
    W = w_out.shape[0]
    T = min(ROW_TILE, S)
    assert S % T == 0 and T % CHUNK == 0
    row = pl.BlockSpec((None, T, D), lambda b, i: (b, i, 0))
    return pl.pallas_call(
        functools.partial(_gmlp_kernel, n_chunks=T // CHUNK, width=W),
        grid=(B, S // T),
        in_specs=[row, _const_spec((1, D)), _const_spec((D, 2 * W)), _const_spec((1, W)),
                  _const_spec((GMLP_GROUPS, CHUNK, CHUNK)), _const_spec((CHUNK, W)),
                  _const_spec((W, D))],
        out_specs=row,
        out_shape=jax.ShapeDtypeStruct((B, S, D), F32),
        scratch_shapes=[pltpu.VMEM((T, W), BF16)],
        compiler_params=_params(2),
        name="gmlp_layer",
    )(x, g_mix, w_in, v_gain, w_s, b_full, w_out)


def _qkv_kernel(x_ref, g_ref, wqv_ref, wk_ref, cq_ref, sq_ref, ck_ref, sk_ref,
                qa_ref, qb_ref, k_ref, vt_ref, *, d_model):
    hb = _rms(x_ref[...], g_ref[...], NORM_EPS).astype(BF16)
    qv_t = lax.dot_general(wqv_ref[...], hb, (((1,), (1,)), ((), ())),
                           preferred_element_type=F32)
    kk = _dot(hb, wk_ref[...])
    cq, sq = cq_ref[...], sq_ref[...]
    ck, sk = ck_ref[...], sk_ref[...]
    scale = DIFF_HEAD_DIM ** -0.5 * math.log2(math.e)
    half = DIFF_HEAD_DIM // 2
    zero = jnp.zeros((half, hb.shape[0]), F32)
    ones = jnp.ones((SUM_ROWS, hb.shape[0]), F32)
    for h in range(DIFF_HEADS):
        xh = qv_t[h * DIFF_V_DIM:(h + 1) * DIFF_V_DIM]
        x1, x2 = xh[:DIFF_HEAD_DIM], xh[DIFF_HEAD_DIM:]
        o1 = (x1 * cq - x2 * sq) * scale
        o2 = (x2 * cq + x1 * sq) * scale
        qa_ref[h] = jnp.concatenate([o1[:half], zero, o2[:half], zero], axis=0).astype(BF16)
        qb_ref[h] = jnp.concatenate([zero, o1[half:], zero, o2[half:]], axis=0).astype(BF16)
        vt_ref[h, 0] = jnp.concatenate(
            [qv_t[d_model + h * DIFF_V_DIM:d_model + (h + 1) * DIFF_V_DIM], ones], axis=0).astype(BF16)
        kh = kk[:, h * DIFF_V_DIM:(h + 1) * DIFF_V_DIM]
        k_ref[h] = (kh * ck + pltpu.roll(kh, DIFF_HEAD_DIM, 1) * sk).astype(BF16)


def _qkv_rope(x, g_mix, wqv_t, wk, tables, T):
    B, S, D = x.shape
    H = DIFF_HEADS
    cq, sq, ck, sk = tables
    hd = DIFF_V_DIM
    q_spec = pl.BlockSpec((None, H, hd, T), lambda b, i: (b, 0, 0, i))
    return pl.pallas_call(
        functools.partial(_qkv_kernel, d_model=D),
        grid=(B, S // T),
        in_specs=[pl.BlockSpec((None, T, D), lambda b, i: (b, i, 0)),
                  _const_spec((1, D)), _const_spec((2 * D, D)), _const_spec((D, D)),
                  pl.BlockSpec((DIFF_HEAD_DIM, T), lambda b, i: (0, i)),
                  pl.BlockSpec((DIFF_HEAD_DIM, T), lambda b, i: (0, i)),
                  pl.BlockSpec((T, hd), lambda b, i: (i, 0)),
                  pl.BlockSpec((T, hd), lambda b, i: (i, 0))],
        out_specs=[q_spec, q_spec,
                   pl.BlockSpec((None, H, T, hd), lambda b, i: (b, 0, i, 0)),
                   pl.BlockSpec((None, H, 1, hd + SUM_ROWS, T), lambda b, i: (b, 0, i, 0, 0))],
        out_shape=[jax.ShapeDtypeStruct((B, H, hd, S), BF16),
                   jax.ShapeDtypeStruct((B, H, hd, S), BF16),
                   jax.ShapeDtypeStruct((B, H, S, hd), BF16),
                   jax.ShapeDtypeStruct((B, H, S // T, hd + SUM_ROWS, T), BF16)],
        compiler_params=_params(2),
        name="qkv_rope",
    )(x, g_mix, wqv_t, wk, cq, sq, ck, sk)


def _attn_kernel(qa_ref, qb_ref, k_ref, vt_ref, lq1_ref, lk1_ref, lq2_ref, lk2_ref, g_ref,
                 o_ref, sa_ref, sb_ref, pa_ref, pb_ref, acca_ref, accb_ref, *, tk, lam_init):
    qa, qb = qa_ref[...], qb_ref[...]
    tq = qa.shape[1]
    n_k = k_ref.shape[0] // tk

    def k_block(i):
        return k_ref[pl.ds(pl.multiple_of(i * tk, tk), tk), :]

    def col_max(s):
        part = jnp.max(s.reshape(s.shape[0] // SUBLANES, SUBLANES, s.shape[1]), axis=0)
        return jnp.max(part, axis=0, keepdims=True)

    def scores(i, q, s_ref):
        s = _dot(k_block(i), q)
        s_ref[...] = s
        return col_max(s)

    def mxu_step(k_idx, q, s_ref, v_idx, p_ref, acc_ref, alpha):
        k, vt = k_block(k_idx), vt_ref[v_idx]
        mx = []
        for c in range(0, tq, MXU_WIDTH):
            cols = slice(c, c + MXU_WIDTH)
            s = _dot(k, q[:, cols])
            s_ref[:, cols] = s
            mx.append(col_max(s))
            acc_ref[:, cols] = acc_ref[:, cols] * alpha[:, cols] + _dot(vt, p_ref[:, cols])
        return jnp.concatenate(mx, axis=1)

    def softmax_step(s_ref, p_ref, m_blk, m):
        m_new = jnp.maximum(m, m_blk)
        for r in range(0, tk, SOFTMAX_ROWS):
            p_ref[r:r + SOFTMAX_ROWS, :] = jnp.exp2(s_ref[r:r + SOFTMAX_ROWS, :] - m_new).astype(BF16)
        return m_new, jnp.exp2(m - m_new)

    def half_step(j, u, carry):
        mxa, ma, mb, alpha_b = carry
        w = 1 - u
        mxb = mxu_step(j, qb, sb_ref.at[u], jnp.maximum(j - 1, 0), pb_ref.at[w], accb_ref, alpha_b)
        ma, alpha_a = softmax_step(sa_ref.at[u], pa_ref.at[u], mxa, ma)
        mxa = mxu_step(jnp.minimum(j + 1, n_k - 1), qa, sa_ref.at[w], j, pa_ref.at[u], acca_ref, alpha_a)
        mb, alpha_b = softmax_step(sb_ref.at[u], pb_ref.at[u], mxb, mb)
        return mxa, ma, mb, alpha_b

    mxa0 = scores(0, qa, sa_ref.at[0])
    pb_ref[1] = jnp.zeros((tk, tq), BF16)
    acca_ref[...] = jnp.zeros_like(acca_ref)
    accb_ref[...] = jnp.zeros_like(accb_ref)
    neg = jnp.full((1, tq), -jnp.inf, F32)
    _, _, _, alpha_b = lax.fori_loop(
        0, n_k // 2, lambda i, c: half_step(2 * i + 1, 1, half_step(2 * i, 0, c)),
        (mxa0, neg, neg, jnp.ones((1, tq), F32)), unroll=2)
    acc_a = acca_ref[...]
    acc_b = accb_ref[...] * alpha_b + _dot(vt_ref[n_k - 1], pb_ref[1])
    hd = DIFF_V_DIM
    la, lb = acc_a[hd:hd + 1], acc_b[hd:hd + 1]
    lam = (jnp.exp(jnp.sum(lq1_ref[...] * lk1_ref[...], axis=1, keepdims=True))
           - jnp.exp(jnp.sum(lq2_ref[...] * lk2_ref[...], axis=1, keepdims=True)) + lam_init)
    o = acc_a[:hd] / la - lam * (acc_b[:hd] / lb)
    ms = jnp.mean(o * o, axis=0, keepdims=True)
    o = o * lax.rsqrt(ms + SUBLN_EPS) * g_ref[...] * (1.0 - lam_init)
    o_ref[...] = o.T.astype(BF16)


def _diff_attn(qa, qb, k, vt, lam_q1, lam_k1, lam_q2, lam_k2, subln_g, lam_init, tq):
    B, H, hd, S = qa.shape
    n_kb, vt_rows, tk = vt.shape[2:]
    assert n_kb % 2 == 0 and S % tq == 0
    q_spec = pl.BlockSpec((None, None, hd, tq), lambda b, h, i: (b, h, 0, i))
    lam_spec = _const_spec((1, DIFF_HEAD_DIM))
    return pl.pallas_call(
        functools.partial(_attn_kernel, tk=tk, lam_init=lam_init),
        grid=(B, H, S // tq),
        in_specs=[q_spec, q_spec,
                  pl.BlockSpec((None, None, S, hd), lambda b, h, i: (b, h, 0, 0)),
                  pl.BlockSpec((None, None, n_kb, vt_rows, tk), lambda b, h, i: (b, h, 0, 0, 0)),
                  lam_spec, lam_spec, lam_spec, lam_spec, _const_spec((hd, 1))],
        out_specs=pl.BlockSpec((None, tq, hd), lambda b, h, i: (b, i, h)),
        out_shape=jax.ShapeDtypeStruct((B, S, H * hd), BF16),
        scratch_shapes=[pltpu.VMEM((2, tk, tq), F32), pltpu.VMEM((2, tk, tq), F32),
                        pltpu.VMEM((2, tk, tq), BF16), pltpu.VMEM((2, tk, tq), BF16),
                        pltpu.VMEM((vt_rows, tq), F32), pltpu.VMEM((vt_rows, tq), F32)],
        compiler_params=_params(3),
        name="diff_attn",
    )(qa, qb, k, vt, lam_q1, lam_k1, lam_q2, lam_k2, subln_g)


def _ffn_kernel(*refs, tile, n_chunk, has_proj, final_norm):
    refs = list(refs)
    x_ref, xp_ref, xn_ref = refs[:3]
    del refs[:3]
    if has_proj:
        o_ref, op_ref, on_ref, wo_ref = refs[:4]
        del refs[:4]
    gn_ref, win_ref, cw_ref, cb_ref, wout_ref = refs[:5]
    del refs[:5]
    if final_norm:
        gf_ref = refs.pop(0)
    out_ref, y_ref = refs

    halo = SUBLANES
    rows = tile + 2 * halo
    i, n = pl.program_id(1), pl.num_programs(1)
    xa = jnp.concatenate([xp_ref[...], x_ref[...], xn_ref[...]], axis=0)
    if has_proj:
        oa = jnp.concatenate([op_ref[...], o_ref[...], on_ref[...]], axis=0)
        xa = xa + _dot(oa, wo_ref[...])
    h = _rms(xa, gn_ref[...], NORM_EPS)
    r = lax.broadcasted_iota(jnp.int32, (rows, 1), 0)
    inside = ((r >= halo) | (i > 0)) & ((r < tile + halo) | (i < n - 1))
    hb = jnp.where(inside, h, 0.0).astype(BF16)
    n_vr = tile // SUBLANES
    sub = lax.broadcasted_iota(jnp.int32, (1, SUBLANES, 1), 1)
    for j in range(n_chunk):
        a = _dot(hb, win_ref[j])
        cw = cw_ref[j]
        a3 = a.reshape(n_vr + 2, SUBLANES, 2 * FFN_CHUNK)
        dn = pltpu.roll(a3, 1, 1)
        up = pltpu.roll(a3, SUBLANES - 1, 1)
        a_prev = jnp.where(sub == 0, dn[0:n_vr], dn[1:n_vr + 1])
        a_next = jnp.where(sub == SUBLANES - 1, up[2:n_vr + 2], up[1:n_vr + 1])
        c = cb_ref[j] + a_prev * cw[0:1] + a3[1:n_vr + 1] * cw[1:2] + a_next * cw[2:3]
        c = c.reshape(tile, 2 * FFN_CHUNK)
        gate, lin = c[:, :FFN_CHUNK], c[:, FFN_CHUNK:]
        y_ref[:, j * FFN_CHUNK:(j + 1) * FFN_CHUNK] = (
            gate * (1.0 / (1.0 + jnp.exp(-gate))) * lin).astype(BF16)
    out = xa[halo:halo + tile] + _dot(y_ref[...], wout_ref[...])
    if final_norm:
        out = _rms(out, gf_ref[...], NORM_EPS)
    out_ref[...] = out


def _conv_ffn(x, g_ffn, w_in, conv_w, conv_b, w_out, proj=None, g_final=None):
    B, S, D = x.shape
    T = min(ROW_TILE, S)
    halo = SUBLANES
    n_chunk = w_in.shape[0]
    assert S % T == 0 and T % halo == 0
    tb, last = T // halo, S // halo - 1

    def triple():
        return [pl.BlockSpec((None, T, D), lambda b, i: (b, i, 0)),
                pl.BlockSpec((None, halo, D), lambda b, i: (b, jnp.maximum(i * tb - 1, 0), 0)),
                pl.BlockSpec((None, halo, D), lambda b, i: (b, jnp.minimum((i + 1) * tb, last), 0))]

    args, specs = [x, x, x], triple()
    if proj is not None:
        o, w_o = proj
        args += [o, o, o, w_o]
        specs += triple() + [_const_spec(w_o.shape)]
    args += [g_ffn, w_in, conv_w, conv_b, w_out]
    specs += [_const_spec((1, D)), _const_spec(w_in.shape), _const_spec(conv_w.shape),
              _const_spec(conv_b.shape), _const_spec(w_out.shape)]
    if g_final is not None:
        args.append(g_final)
        specs.append(_const_spec((1, D)))
    return pl.pallas_call(
        functools.partial(_ffn_kernel, tile=T, n_chunk=n_chunk, has_proj=proj is not None,
                          final_norm=g_final is not None),
        grid=(B, S // T),
        in_specs=specs,
        out_specs=pl.BlockSpec((None, T, D), lambda b, i: (b, i, 0)),
        out_shape=jax.ShapeDtypeStruct((B, S, D), F32),
        scratch_shapes=[pltpu.VMEM((T, n_chunk * FFN_CHUNK), BF16)],
        compiler_params=_params(2),
        name="conv_ffn",
    )(*args)


def _rope_tables(S):
    half = DIFF_HEAD_DIM // 2
    pos = jnp.arange(S, dtype=F32)
    inv_freq = ROPE_THETA ** (-jnp.arange(0, DIFF_HEAD_DIM, 2, dtype=F32) / DIFF_HEAD_DIM)
    ang = pos[:, None] * inv_freq[None, :]
    cos, sin = jnp.cos(ang), jnp.sin(ang)
    cq = jnp.concatenate([cos, cos], axis=1).T
    sq = jnp.concatenate([sin, sin], axis=1).T
    ck = jnp.concatenate([cos] * 4, axis=1)
    sk = jnp.concatenate([-sin, -sin, sin, sin], axis=1)
    return cq, sq, ck, sk


def _head_feature_order():
    half = DIFF_HEAD_DIM // 2
    n = jnp.arange(DIFF_V_DIM)
    part, c, i = n // DIFF_HEAD_DIM, (n % DIFF_HEAD_DIM) // half, n % half
    within = c * DIFF_HEAD_DIM + part * half + i
    return (jnp.arange(DIFF_HEADS)[:, None] * DIFF_V_DIM + within[None, :]).reshape(-1)


def _prep_ffn(w_in, conv_w, conv_b, w_out):
    D, F2 = w_in.shape
    F = F2 // 2
    assert F % FFN_CHUNK == 0
    n = F // FFN_CHUNK

    def pair(a):
        g = a[..., :F].reshape(a.shape[:-1] + (n, FFN_CHUNK))
        u = a[..., F:].reshape(a.shape[:-1] + (n, FFN_CHUNK))
        return jnp.moveaxis(jnp.concatenate([g, u], axis=-1), -2, 0)

    return (pair(w_in).astype(BF16), pair(conv_w), pair(conv_b[None, :]),
            w_out.astype(BF16))


def _trunk(x, p):
    B, S, D = x.shape
    depth = p["norm_mix"].shape[0]
    tables = _rope_tables(S)
    pending = None
    for i in range(depth):
        j = i // 2
        g_mix = p["norm_mix"][i][None, :]
        if i % 2 == 0:
            x = _gmlp_layer(x, g_mix, *p["gmlp"][j])
        else:
            wqv_t, wk, w_o, lam, subln_g = p["diff"][j]
            qa, qb, k, vt = _qkv_rope(x, g_mix, wqv_t, wk, tables, min(K_TILE, S))
            lam_init = 0.8 - 0.6 * math.exp(-0.3 * i)
            o = _diff_attn(qa, qb, k, vt, *lam, subln_g, lam_init, min(Q_TILE, S))
            pending = (o, w_o)
        g_final = p["norm_final"][None, :] if i == depth - 1 else None
        x = _conv_ffn(x, p["norm_ffn"][i][None, :], *p["ffn"][i], proj=pending, g_final=g_final)
        pending = None
    return x


def kernel(x_prompt, x_sample, norm_mix, norm_ffn, norm_final, gmlp_w_in, gmlp_v_gain, gmlp_w_s, gmlp_b_s, gmlp_w_out, diff_w_qkv, diff_lam_q1, diff_lam_k1, diff_lam_q2, diff_lam_k2, diff_subln_g, diff_w_out, ffn_w_in, ffn_conv_w, ffn_conv_b, ffn_w_out):
    D = x_prompt.shape[-1]
    order = _head_feature_order()
    gmlp = []
    for j in range(gmlp_w_in.shape[0]):
        b_full = jnp.repeat(gmlp_b_s[j].T, CHUNK, axis=1)
        gmlp.append((gmlp_w_in[j].astype(BF16), gmlp_v_gain[j][None, :], gmlp_w_s[j].astype(BF16),
                     b_full, gmlp_w_out[j].astype(BF16)))
    diff = []
    for j in range(diff_w_qkv.shape[0]):
        w = diff_w_qkv[j]
        wq, wk, wv = w[:, :D][:, order], w[:, D:2 * D][:, order], w[:, 2 * D:]
        wqv_t = jnp.concatenate([wq, wv], axis=1).T.astype(BF16)
        lam = tuple(a[j][None, :] for a in (diff_lam_q1, diff_lam_k1, diff_lam_q2, diff_lam_k2))
        diff.append((wqv_t, wk.astype(BF16), diff_w_out[j].astype(BF16), lam,
                     diff_subln_g[j][:, None]))
    ffn = [_prep_ffn(ffn_w_in[i], ffn_conv_w[i], ffn_conv_b[i], ffn_w_out[i])
           for i in range(ffn_w_in.shape[0])]
    p = dict(norm_mix=norm_mix, norm_ffn=norm_ffn, norm_final=norm_final,
             gmlp=gmlp, diff=diff, ffn=ffn)
    return (_trunk(x_prompt, p), _trunk(x_sample, p))
```

```python
import functools
import math

import jax
import jax.numpy as jnp
from jax import lax
from jax.experimental import pallas as pl
from jax.experimental.pallas import tpu as pltpu

CHUNK = 128
GMLP_GROUPS = 8
DIFF_HEADS = 8
DIFF_HEAD_DIM = 64
DIFF_V_DIM = 2 * DIFF_HEAD_DIM
CONV_WIDTH = 3
ROPE_THETA = 10000.0
NORM_EPS = 1e-6
SUBLN_EPS = 1e-5

LANES = 128
SUBLANES = 8
MXU_WIDTH = 256
ROW_TILE = 512
GMLP_TILE = 1024
QKV_TILE = 1024
Q_TILE = 2048
K_TILE = 512
SOFTMAX_ROWS = 64
SUM_ROWS = 16
FFN_CHUNK = MXU_WIDTH
VMEM_LIMIT = 56 * 1024 * 1024

F32 = jnp.float32
BF16 = jnp.bfloat16


def _dot(a, b):
    return jnp.dot(a, b, preferred_element_type=F32)


def _rms(x, g, eps):
    ms = jnp.mean(x * x, axis=-1, keepdims=True)
    return x * lax.rsqrt(ms + eps) * g


def _const_spec(shape):
    zeros = (0,) * len(shape)
    return pl.BlockSpec(shape, lambda *_: zeros, pipeline_mode=pl.Buffered(1))


def _params(n_grid, flags=None):
    return pltpu.CompilerParams(
        dimension_semantics=("arbitrary",) * n_grid, vmem_limit_bytes=VMEM_LIMIT, flags=flags)


def _gmlp_kernel(x_ref, g_ref, win_ref, vg_ref, ws_ref, bs_ref, wout_ref, o_ref, y_ref,
                 *, n_chunks, width):
    x = x_ref[...]
    hb = _rms(x, g_ref[...], NORM_EPS).astype(BF16)
    z = _dot(hb, win_ref[...])
    z = 0.5 * z * (1.0 + jnp.tanh(math.sqrt(2.0 / math.pi) * (z + 0.044715 * (z * z * z))))
    u = z[:, :width]
    v = _rms(z[:, width:], vg_ref[...], NORM_EPS).astype(BF16)
    for g in range(GMLP_GROUPS):
        cols = slice(g * CHUNK, (g + 1) * CHUNK)
        vcat = jnp.concatenate(
            [v[c * CHUNK:(c + 1) * CHUNK, cols] for c in range(n_chunks)], axis=1)
        r = _dot(ws_ref[g], vcat)
        b = bs_ref[:, cols]
        for c in range(n_chunks):
            rows = slice(c * CHUNK, (c + 1) * CHUNK)
            y_ref[rows, cols] = (u[rows, cols] * (r[:, c * CHUNK:(c + 1) * CHUNK] + b)).astype(BF16)
    o_ref[...] = x + _dot(y_ref[...], wout_ref[...])


def _gmlp_layer(x, g_mix, w_in, v_gain, w_s, b_full, w_out):
    B, S, D = x.shape
    W = w_out.shape[0]
    T = min(GMLP_TILE, S)
    assert S % T == 0 and T % CHUNK == 0
    row = pl.BlockSpec((None, T, D), lambda b, i: (b, i, 0))
    return pl.pallas_call(
        functools.partial(_gmlp_kernel, n_chunks=T // CHUNK, width=W),
        grid=(B, S // T),
        in_specs=[row, _const_spec((1, D)), _const_spec((D, 2 * W)), _const_spec((1, W)),
                  _const_spec((GMLP_GROUPS, CHUNK, CHUNK)), _const_spec((CHUNK, W)),
                  _const_spec((W, D))],
        out_specs=row,
        out_shape=jax.ShapeDtypeStruct((B, S, D), F32),
        scratch_shapes=[pltpu.VMEM((T, W), BF16)],
        compiler_params=_params(2),
        name="gmlp_layer",
    )(x, g_mix, w_in, v_gain, w_s, b_full, w_out)


def _qkv_kernel(x_ref, g_ref, wqv_ref, wk_ref, cq_ref, sq_ref, ck_ref, sk_ref,
                qa_ref, qb_ref, k_ref, vt_ref, *, d_model, tk):
    hb = _rms(x_ref[...], g_ref[...], NORM_EPS).astype(BF16)
    qv_t = lax.dot_general(wqv_ref[...], hb, (((1,), (1,)), ((), ())),
                           preferred_element_type=F32)
    kk = _dot(hb, wk_ref[...])
    cq, sq = cq_ref[...], sq_ref[...]
    ck, sk = ck_ref[...], sk_ref[...]
    scale = DIFF_HEAD_DIM ** -0.5 * math.log2(math.e)
    half = DIFF_HEAD_DIM // 2
    zero = jnp.zeros((half, hb.shape[0]), F32)
    ones = jnp.ones((SUM_ROWS, hb.shape[0]), F32)
    for h in range(DIFF_HEADS):
        xh = qv_t[h * DIFF_V_DIM:(h + 1) * DIFF_V_DIM]
        x1, x2 = xh[:DIFF_HEAD_DIM], xh[DIFF_HEAD_DIM:]
        o1 = (x1 * cq - x2 * sq) * scale
        o2 = (x2 * cq + x1 * sq) * scale
        qa_ref[h] = jnp.concatenate([o1[:half], zero, o2[:half], zero], axis=0).astype(BF16)
        qb_ref[h] = jnp.concatenate([zero, o1[half:], zero, o2[half:]], axis=0).astype(BF16)
        vt = jnp.concatenate(
            [qv_t[d_model + h * DIFF_V_DIM:d_model + (h + 1) * DIFF_V_DIM], ones], axis=0).astype(BF16)
        for kb in range(vt_ref.shape[1]):
            vt_ref[h, kb] = vt[:, kb * tk:(kb + 1) * tk]
        kh = kk[:, h * DIFF_V_DIM:(h + 1) * DIFF_V_DIM]
        k_ref[h] = (kh * ck + pltpu.roll(kh, DIFF_HEAD_DIM, 1) * sk).astype(BF16)


def _qkv_rope(x, g_mix, wqv_t, wk, tables, tk):
    B, S, D = x.shape
    T = min(QKV_TILE, S)
    assert S % T == 0 and T % tk == 0
    H = DIFF_HEADS
    cq, sq, ck, sk = tables
    hd = DIFF_V_DIM
    q_spec = pl.BlockSpec((None, H, hd, T), lambda b, i: (b, 0, 0, i))
    return pl.pallas_call(
        functools.partial(_qkv_kernel, d_model=D, tk=tk),
        grid=(B, S // T),
        in_specs=[pl.BlockSpec((None, T, D), lambda b, i: (b, i, 0)),
                  _const_spec((1, D)), _const_spec((2 * D, D)), _const_spec((D, D)),
                  pl.BlockSpec((DIFF_HEAD_DIM, T), lambda b, i: (0, i)),
                  pl.BlockSpec((DIFF_HEAD_DIM, T), lambda b, i: (0, i)),
                  pl.BlockSpec((T, hd), lambda b, i: (i, 0)),
                  pl.BlockSpec((T, hd), lambda b, i: (i, 0))],
        out_specs=[q_spec, q_spec,
                   pl.BlockSpec((None, H, T, hd), lambda b, i: (b, 0, i, 0)),
                   pl.BlockSpec((None, H, T // tk, hd + SUM_ROWS, tk), lambda b, i: (b, 0, i, 0, 0))],
        out_shape=[jax.ShapeDtypeStruct((B, H, hd, S), BF16),
                   jax.ShapeDtypeStruct((B, H, hd, S), BF16),
                   jax.ShapeDtypeStruct((B, H, S, hd), BF16),
                   jax.ShapeDtypeStruct((B, H, S // tk, hd + SUM_ROWS, tk), BF16)],
        compiler_params=_params(2),
        name="qkv_rope",
    )(x, g_mix, wqv_t, wk, cq, sq, ck, sk)


def _attn_kernel(qa_ref, qb_ref, k_ref, vt_ref, lq1_ref, lk1_ref, lq2_ref, lk2_ref, g_ref,
                 o_ref, sa_ref, sb_ref, pa_ref, pb_ref, acca_ref, accb_ref, *, tk, lam_init):
    qa, qb = qa_ref[...], qb_ref[...]
    tq = qa.shape[1]
    n_k = k_ref.shape[0] // tk

    def k_block(i):
        return k_ref[pl.ds(pl.multiple_of(i * tk, tk), tk), :]

    def col_max(s):
        part = jnp.max(s.reshape(s.shape[0] // SUBLANES, SUBLANES, s.shape[1]), axis=0)
        return jnp.max(part, axis=0, keepdims=True)

    def scores(i, q, s_ref):
        s = _dot(k_block(i), q)
        s_ref[...] = s
        return col_max(s)

    def mxu_step(k_idx, q, s_ref, v_idx, p_ref, acc_ref, alpha):
        k, vt = k_block(k_idx), vt_ref[v_idx]
        mx = []
        for c in range(0, tq, MXU_WIDTH):
            cols = slice(c, c + MXU_WIDTH)
            s = _dot(k, q[:, cols])
            s_ref[:, cols] = s
            mx.append(col_max(s))
            acc_ref[:, cols] = acc_ref[:, cols] * alpha[:, cols] + _dot(vt, p_ref[:, cols])
        return jnp.concatenate(mx, axis=1)

    def softmax_step(s_ref, p_ref, m_blk, m):
        m_new = jnp.maximum(m, m_blk)
        for r in range(0, tk, SOFTMAX_ROWS):
            p_ref[r:r + SOFTMAX_ROWS, :] = jnp.exp2(s_ref[r:r + SOFTMAX_ROWS, :] - m_new).astype(BF16)
        return m_new, jnp.exp2(m - m_new)

    def half_step(j, u, carry):
        mxa, ma, mb, alpha_b = carry
        w = 1 - u
        mxb = mxu_step(j, qb, sb_ref.at[u], jnp.maximum(j - 1, 0), pb_ref.at[w], accb_ref, alpha_b)
        ma, alpha_a = softmax_step(sa_ref.at[u], pa_ref.at[u], mxa, ma)
        mxa = mxu_step(jnp.minimum(j + 1, n_k - 1), qa, sa_ref.at[w], j, pa_ref.at[u], acca_ref, alpha_a)
        mb, alpha_b = softmax_step(sb_ref.at[u], pb_ref.at[u], mxb, mb)
        return mxa, ma, mb, alpha_b

    mxa0 = scores(0, qa, sa_ref.at[0])
    pb_ref[1] = jnp.zeros((tk, tq), BF16)
    acca_ref[...] = jnp.zeros_like(acca_ref)
    accb_ref[...] = jnp.zeros_like(accb_ref)
    neg = jnp.full((1, tq), -jnp.inf, F32)
    _, _, _, alpha_b = lax.fori_loop(
        0, n_k // 2, lambda i, c: half_step(2 * i + 1, 1, half_step(2 * i, 0, c)),
        (mxa0, neg, neg, jnp.ones((1, tq), F32)), unroll=2)
    acc_a = acca_ref[...]
    acc_b = accb_ref[...] * alpha_b + _dot(vt_ref[n_k - 1], pb_ref[1])
    hd = DIFF_V_DIM
    la, lb = acc_a[hd:hd + 1], acc_b[hd:hd + 1]
    lam = (jnp.exp(jnp.sum(lq1_ref[...] * lk1_ref[...], axis=1, keepdims=True))
           - jnp.exp(jnp.sum(lq2_ref[...] * lk2_ref[...], axis=1, keepdims=True)) + lam_init)
    o = acc_a[:hd] * (1.0 / la) - acc_b[:hd] * (lam / lb)
    ms = jnp.mean(o * o, axis=0, keepdims=True)
    o = o * (lax.rsqrt(ms + SUBLN_EPS) * (1.0 - lam_init)) * g_ref[...]
    o_ref[...] = o.T.astype(BF16)


def _diff_attn(qa, qb, k, vt, lam_q1, lam_k1, lam_q2, lam_k2, subln_g, lam_init, tq):
    B, H, hd, S = qa.shape
    n_kb, vt_rows, tk = vt.shape[2:]
    assert n_kb % 2 == 0 and S % tq == 0
    q_spec = pl.BlockSpec((None, None, hd, tq), lambda b, h, i: (b, h, 0, i))
    lam_spec = _const_spec((1, DIFF_HEAD_DIM))
    return pl.pallas_call(
        functools.partial(_attn_kernel, tk=tk, lam_init=lam_init),
        grid=(B, H, S // tq),
        in_specs=[q_spec, q_spec,
                  pl.BlockSpec((None, None, S, hd), lambda b, h, i: (b, h, 0, 0)),
                  pl.BlockSpec((None, None, n_kb, vt_rows, tk), lambda b, h, i: (b, h, 0, 0, 0)),
                  lam_spec, lam_spec, lam_spec, lam_spec, _const_spec((hd, 1))],
        out_specs=pl.BlockSpec((None, tq, hd), lambda b, h, i: (b, i, h)),
        out_shape=jax.ShapeDtypeStruct((B, S, H * hd), BF16),
        scratch_shapes=[pltpu.VMEM((2, tk, tq), F32), pltpu.VMEM((2, tk, tq), F32),
                        pltpu.VMEM((2, tk, tq), BF16), pltpu.VMEM((2, tk, tq), BF16),
                        pltpu.VMEM((vt_rows, tq), F32), pltpu.VMEM((vt_rows, tq), F32)],
        compiler_params=_params(3),
        name="diff_attn",
    )(qa, qb, k, vt, lam_q1, lam_k1, lam_q2, lam_k2, subln_g)


def _ffn_kernel(*refs, tile, n_chunk, has_proj, final_norm):
    refs = list(refs)
    x_ref, xp_ref, xn_ref = refs[:3]
    del refs[:3]
    if has_proj:
        o_ref, op_ref, on_ref, wo_ref = refs[:4]
        del refs[:4]
    gn_ref, win_ref, cw_ref, cb_ref, wout_ref = refs[:5]
    del refs[:5]
    if final_norm:
        gf_ref = refs.pop(0)
    out_ref, y_ref = refs

    halo = SUBLANES
    rows = tile + 2 * halo
    i, n = pl.program_id(1), pl.num_programs(1)
    xa = jnp.concatenate([xp_ref[...], x_ref[...], xn_ref[...]], axis=0)
    if has_proj:
        oa = jnp.concatenate([op_ref[...], o_ref[...], on_ref[...]], axis=0)
        xa = xa + _dot(oa, wo_ref[...])
    h = _rms(xa, gn_ref[...], NORM_EPS)
    r = lax.broadcasted_iota(jnp.int32, (rows, 1), 0)
    inside = ((r >= halo) | (i > 0)) & ((r < tile + halo) | (i < n - 1))
    hb = jnp.where(inside, h, 0.0).astype(BF16)
    n_vr = tile // SUBLANES
    sub = lax.broadcasted_iota(jnp.int32, (1, SUBLANES, 1), 1)
    for j in range(n_chunk):
        a = _dot(hb, win_ref[j])
        cw = cw_ref[j]
        a3 = a.reshape(n_vr + 2, SUBLANES, 2 * FFN_CHUNK)
        dn = pltpu.roll(a3, 1, 1)
        up = pltpu.roll(a3, SUBLANES - 1, 1)
        a_prev = jnp.where(sub == 0, dn[0:n_vr], dn[1:n_vr + 1])
        a_next = jnp.where(sub == SUBLANES - 1, up[2:n_vr + 2], up[1:n_vr + 1])
        c = cb_ref[j] + a_prev * cw[0:1] + a3[1:n_vr + 1] * cw[1:2] + a_next * cw[2:3]
        c = c.reshape(tile, 2 * FFN_CHUNK)
        gate, lin = c[:, :FFN_CHUNK], c[:, FFN_CHUNK:]
        y_ref[:, j * FFN_CHUNK:(j + 1) * FFN_CHUNK] = (
            gate * (1.0 / (1.0 + jnp.exp(-gate))) * lin).astype(BF16)
    out = xa[halo:halo + tile] + _dot(y_ref[...], wout_ref[...])
    if final_norm:
        out = _rms(out, gf_ref[...], NORM_EPS)
    out_ref[...] = out


def _conv_ffn(x, g_ffn, w_in, conv_w, conv_b, w_out, proj=None, g_final=None):
    B, S, D = x.shape
    T = min(ROW_TILE, S)
    halo = SUBLANES
    n_chunk = w_in.shape[0]
    assert S % T == 0 and T % halo == 0
    tb, last = T // halo, S // halo - 1

    def triple():
        return [pl.BlockSpec((None, T, D), lambda b, i: (b, i, 0)),
                pl.BlockSpec((None, halo, D), lambda b, i: (b, jnp.maximum(i * tb - 1, 0), 0)),
                pl.BlockSpec((None, halo, D), lambda b, i: (b, jnp.minimum((i + 1) * tb, last), 0))]

    args, specs = [x, x, x], triple()
    if proj is not None:
        o, w_o = proj
        args += [o, o, o, w_o]
        specs += triple() + [_const_spec(w_o.shape)]
    args += [g_ffn, w_in, conv_w, conv_b, w_out]
    specs += [_const_spec((1, D)), _const_spec(w_in.shape), _const_spec(conv_w.shape),
              _const_spec(conv_b.shape), _const_spec(w_out.shape)]
    if g_final is not None:
        args.append(g_final)
        specs.append(_const_spec((1, D)))
    return pl.pallas_call(
        functools.partial(_ffn_kernel, tile=T, n_chunk=n_chunk, has_proj=proj is not None,
                          final_norm=g_final is not None),
        grid=(B, S // T),
        in_specs=specs,
        out_specs=pl.BlockSpec((None, T, D), lambda b, i: (b, i, 0)),
        out_shape=jax.ShapeDtypeStruct((B, S, D), F32),
        scratch_shapes=[pltpu.VMEM((T, n_chunk * FFN_CHUNK), BF16)],
        compiler_params=_params(2),
        name="conv_ffn",
    )(*args)


def _rope_tables(S):
    half = DIFF_HEAD_DIM // 2
    pos = jnp.arange(S, dtype=F32)
    inv_freq = ROPE_THETA ** (-jnp.arange(0, DIFF_HEAD_DIM, 2, dtype=F32) / DIFF_HEAD_DIM)
    ang = pos[:, None] * inv_freq[None, :]
    cos, sin = jnp.cos(ang), jnp.sin(ang)
    cq = jnp.concatenate([cos, cos], axis=1).T
    sq = jnp.concatenate([sin, sin], axis=1).T
    ck = jnp.concatenate([cos] * 4, axis=1)
    sk = jnp.concatenate([-sin, -sin, sin, sin], axis=1)
    return cq, sq, ck, sk


def _head_feature_order():
    half = DIFF_HEAD_DIM // 2
    n = jnp.arange(DIFF_V_DIM)
    part, c, i = n // DIFF_HEAD_DIM, (n % DIFF_HEAD_DIM) // half, n % half
    within = c * DIFF_HEAD_DIM + part * half + i
    return (jnp.arange(DIFF_HEADS)[:, None] * DIFF_V_DIM + within[None, :]).reshape(-1)


def _prep_ffn(w_in, conv_w, conv_b, w_out):
    D, F2 = w_in.shape
    F = F2 // 2
    assert F % FFN_CHUNK == 0
    n = F // FFN_CHUNK

    def pair(a):
        g = a[..., :F].reshape(a.shape[:-1] + (n, FFN_CHUNK))
        u = a[..., F:].reshape(a.shape[:-1] + (n, FFN_CHUNK))
        return jnp.moveaxis(jnp.concatenate([g, u], axis=-1), -2, 0)

    return (pair(w_in).astype(BF16), pair(conv_w), pair(conv_b[None, :]),
            w_out.astype(BF16))


def _trunk(x, p):
    B, S, D = x.shape
    depth = p["norm_mix"].shape[0]
    tables = _rope_tables(S)
    pending = None
    for i in range(depth):
        j = i // 2
        g_mix = p["norm_mix"][i][None, :]
        if i % 2 == 0:
            x = _gmlp_layer(x, g_mix, *p["gmlp"][j])
        else:
            wqv_t, wk, w_o, lam, subln_g = p["diff"][j]
            qa, qb, k, vt = _qkv_rope(x, g_mix, wqv_t, wk, tables, min(K_TILE, S))
            lam_init = 0.8 - 0.6 * math.exp(-0.3 * i)
            o = _diff_attn(qa, qb, k, vt, *lam, subln_g, lam_init, min(Q_TILE, S))
            pending = (o, w_o)
        g_final = p["norm_final"][None, :] if i == depth - 1 else None
        x = _conv_ffn(x, p["norm_ffn"][i][None, :], *p["ffn"][i], proj=pending, g_final=g_final)
        pending = None
    return x


def kernel(x_prompt, x_sample, norm_mix, norm_ffn, norm_final, gmlp_w_in, gmlp_v_gain, gmlp_w_s, gmlp_b_s, gmlp_w_out, diff_w_qkv, diff_lam_q1, diff_lam_k1, diff_lam_q2, diff_lam_k2, diff_subln_g, diff_w_out, ffn_w_in, ffn_conv_w, ffn_conv_b, ffn_w_out):
    D = x_prompt.shape[-1]
    order = _head_feature_order()
    gmlp = []
    for j in range(gmlp_w_in.shape[0]):
        b_full = jnp.repeat(gmlp_b_s[j].T, CHUNK, axis=1)
        gmlp.append((gmlp_w_in[j].astype(BF16), gmlp_v_gain[j][None, :], gmlp_w_s[j].astype(BF16),
                     b_full, gmlp_w_out[j].astype(BF16)))
    diff = []
    for j in range(diff_w_qkv.shape[0]):
        w = diff_w_qkv[j]
        wq, wk, wv = w[:, :D][:, order], w[:, D:2 * D][:, order], w[:, 2 * D:]
        wqv_t = jnp.concatenate([wq, wv], axis=1).T.astype(BF16)
        lam = tuple(a[j][None, :] for a in (diff_lam_q1, diff_lam_k1, diff_lam_q2, diff_lam_k2))
        diff.append((wqv_t, wk.astype(BF16), diff_w_out[j].astype(BF16), lam,
                     diff_subln_g[j][:, None]))
    ffn = [_prep_ffn(ffn_w_in[i], ffn_conv_w[i], ffn_conv_b[i], ffn_w_out[i])
           for i in range(ffn_w_in.shape[0])]
    p = dict(norm_mix=norm_mix, norm_ffn=norm_ffn, norm_final=norm_final,
             gmlp=gmlp, diff=diff, ffn=ffn)
    return (_trunk(x_prompt, p), _trunk(x_sample, p))
```

```python
import functools
import math

import jax
import jax.numpy as jnp
from jax import lax
from jax.experimental import pallas as pl
from jax.experimental.pallas import tpu as pltpu

CHUNK = 128
GMLP_GROUPS = 8
DIFF_HEADS = 8
DIFF_HEAD_DIM = 64
DIFF_V_DIM = 2 * DIFF_HEAD_DIM
ROPE_THETA = 10000.0
NORM_EPS = 1e-6
SUBLN_EPS = 1e-5

SUBLANES = 8
MXU_WIDTH = 256
ROW_TILE = 512
GMLP_TILE = 1024
QKV_TILE = 1024
Q_TILE = 2048
K_TILE = 512
SOFTMAX_ROWS = 64
SUM_ROWS = 16
FFN_CHUNK = MXU_WIDTH
VMEM_LIMIT = 56 * 1024 * 1024

F32 = jnp.float32
BF16 = jnp.bfloat16


def _dot(a, b):
    return jnp.dot(a, b, preferred_element_type=F32)


def _rms(x, g, eps):
    ms = jnp.mean(x * x, axis=-1, keepdims=True)
    return x * lax.rsqrt(ms + eps) * g


def _const_spec(shape):
    zeros = (0,) * len(shape)
    return pl.BlockSpec(shape, lambda *_: zeros, pipeline_mode=pl.Buffered(1))


def _params(n_grid):
    return pltpu.CompilerParams(
        dimension_semantics=("arbitrary",) * n_grid, vmem_limit_bytes=VMEM_LIMIT)


def _gmlp_kernel(x_ref, g_ref, win_ref, vg_ref, ws_ref, bs_ref, wout_ref, o_ref, y_ref,
                 *, n_chunks, width):
    x = x_ref[...]
    hb = _rms(x, g_ref[...], NORM_EPS).astype(BF16)
    z = _dot(hb, win_ref[...])
    z = 0.5 * z * (1.0 + jnp.tanh(math.sqrt(2.0 / math.pi) * (z + 0.044715 * (z * z * z))))
    u = z[:, :width]
    v = _rms(z[:, width:], vg_ref[...], NORM_EPS).astype(BF16)
    for g in range(GMLP_GROUPS):
        cols = slice(g * CHUNK, (g + 1) * CHUNK)
        vcat = jnp.concatenate(
            [v[c * CHUNK:(c + 1) * CHUNK, cols] for c in range(n_chunks)], axis=1)
        r = _dot(ws_ref[g], vcat)
        b = bs_ref[:, cols]
        for c in range(n_chunks):
            rows = slice(c * CHUNK, (c + 1) * CHUNK)
            y_ref[rows, cols] = (u[rows, cols] * (r[:, c * CHUNK:(c + 1) * CHUNK] + b)).astype(BF16)
    o_ref[...] = x + _dot(y_ref[...], wout_ref[...])


def _gmlp_layer(x, g_mix, w_in, v_gain, w_s, b_full, w_out):
    B, S, D = x.shape
    W = w_out.shape[0]
    T = min(GMLP_TILE, S)
    assert S % T == 0 and T % CHUNK == 0
    row = pl.BlockSpec((None, T, D), lambda b, i: (b, i, 0))
    return pl.pallas_call(
        functools.partial(_gmlp_kernel, n_chunks=T // CHUNK, width=W),
        grid=(B, S // T),
        in_specs=[row, _const_spec((1, D)), _const_spec((D, 2 * W)), _const_spec((1, W)),
                  _const_spec((GMLP_GROUPS, CHUNK, CHUNK)), _const_spec((CHUNK, W)),
                  _const_spec((W, D))],
        out_specs=row,
        out_shape=jax.ShapeDtypeStruct((B, S, D), F32),
        scratch_shapes=[pltpu.VMEM((T, W), BF16)],
        compiler_params=_params(2),
        name="gmlp_layer",
    )(x, g_mix, w_in, v_gain, w_s, b_full, w_out)


def _qkv_kernel(x_ref, g_ref, wqv_ref, wk_ref, cq_ref, sq_ref, ck_ref, sk_ref,
                qa_ref, qb_ref, k_ref, vt_ref, *, d_model, tk):
    hb = _rms(x_ref[...], g_ref[...], NORM_EPS).astype(BF16)
    qv_t = lax.dot_general(wqv_ref[...], hb, (((1,), (1,)), ((), ())),
                           preferred_element_type=F32)
    kk = _dot(hb, wk_ref[...])
    cq, sq = cq_ref[...], sq_ref[...]
    ck, sk = ck_ref[...], sk_ref[...]
    scale = DIFF_HEAD_DIM ** -0.5 * math.log2(math.e)
    half = DIFF_HEAD_DIM // 2
    zero = jnp.zeros((half, hb.shape[0]), F32)
    ones = jnp.ones((SUM_ROWS, hb.shape[0]), F32)
    for h in range(DIFF_HEADS):
        xh = qv_t[h * DIFF_V_DIM:(h + 1) * DIFF_V_DIM]
        x1, x2 = xh[:DIFF_HEAD_DIM], xh[DIFF_HEAD_DIM:]
        o1 = (x1 * cq - x2 * sq) * scale
        o2 = (x2 * cq + x1 * sq) * scale
        qa_ref[h] = jnp.concatenate([o1[:half], zero, o2[:half], zero], axis=0).astype(BF16)
        qb_ref[h] = jnp.concatenate([zero, o1[half:], zero, o2[half:]], axis=0).astype(BF16)
        vt = jnp.concatenate(
            [qv_t[d_model + h * DIFF_V_DIM:d_model + (h + 1) * DIFF_V_DIM], ones], axis=0).astype(BF16)
        for kb in range(vt_ref.shape[1]):
            vt_ref[h, kb] = vt[:, kb * tk:(kb + 1) * tk]
        kh = kk[:, h * DIFF_V_DIM:(h + 1) * DIFF_V_DIM]
        k_ref[h] = (kh * ck + pltpu.roll(kh, DIFF_HEAD_DIM, 1) * sk).astype(BF16)


def _qkv_rope(x, g_mix, wqv_t, wk, tables, tk):
    B, S, D = x.shape
    T = min(QKV_TILE, S)
    assert S % T == 0 and T % tk == 0
    H = DIFF_HEADS
    cq, sq, ck, sk = tables
    hd = DIFF_V_DIM
    q_spec = pl.BlockSpec((None, H, hd, T), lambda b, i: (b, 0, 0, i))
    return pl.pallas_call(
        functools.partial(_qkv_kernel, d_model=D, tk=tk),
        grid=(B, S // T),
        in_specs=[pl.BlockSpec((None, T, D), lambda b, i: (b, i, 0)),
                  _const_spec((1, D)), _const_spec((2 * D, D)), _const_spec((D, D)),
                  pl.BlockSpec((DIFF_HEAD_DIM, T), lambda b, i: (0, i)),
                  pl.BlockSpec((DIFF_HEAD_DIM, T), lambda b, i: (0, i)),
                  pl.BlockSpec((T, hd), lambda b, i: (i, 0)),
                  pl.BlockSpec((T, hd), lambda b, i: (i, 0))],
        out_specs=[q_spec, q_spec,
                   pl.BlockSpec((None, H, T, hd), lambda b, i: (b, 0, i, 0)),
                   pl.BlockSpec((None, H, T // tk, hd + SUM_ROWS, tk), lambda b, i: (b, 0, i, 0, 0))],
        out_shape=[jax.ShapeDtypeStruct((B, H, hd, S), BF16),
                   jax.ShapeDtypeStruct((B, H, hd, S), BF16),
                   jax.ShapeDtypeStruct((B, H, S, hd), BF16),
                   jax.ShapeDtypeStruct((B, H, S // tk, hd + SUM_ROWS, tk), BF16)],
        compiler_params=_params(2),
        name="qkv_rope",
    )(x, g_mix, wqv_t, wk, cq, sq, ck, sk)


def _attn_kernel(qa_ref, qb_ref, k_ref, vt_ref, lq1_ref, lk1_ref, lq2_ref, lk2_ref, g_ref,
                 o_ref, sa_ref, sb_ref, pa_ref, pb_ref, acca_ref, accb_ref, *, tk, lam_init):
    qa, qb = qa_ref[...], qb_ref[...]
    tq = qa.shape[1]
    n_k = k_ref.shape[0] // tk

    def k_block(i):
        return k_ref[pl.ds(pl.multiple_of(i * tk, tk), tk), :]

    def col_max(s):
        part = jnp.max(s.reshape(s.shape[0] // SUBLANES, SUBLANES, s.shape[1]), axis=0)
        return jnp.max(part, axis=0, keepdims=True)

    def scores(i, q, s_ref):
        s = _dot(k_block(i), q)
        s_ref[...] = s
        return col_max(s)

    def mxu_step(k_idx, q, s_ref, v_idx, p_ref, acc_ref, alpha):
        k, vt = k_block(k_idx), vt_ref[v_idx]
        mx = []
        for c in range(0, tq, MXU_WIDTH):
            cols = slice(c, c + MXU_WIDTH)
            s = _dot(k, q[:, cols])
            s_ref[:, cols] = s
            mx.append(col_max(s))
            acc_ref[:, cols] = acc_ref[:, cols] * alpha[:, cols] + _dot(vt, p_ref[:, cols])
        return jnp.concatenate(mx, axis=1)

    def softmax_step(s_ref, p_ref, m_blk, m):
        m_new = jnp.maximum(m, m_blk)
        for r in range(0, tk, SOFTMAX_ROWS):
            p_ref[r:r + SOFTMAX_ROWS, :] = jnp.exp2(s_ref[r:r + SOFTMAX_ROWS, :] - m_new).astype(BF16)
        return m_new, jnp.exp2(m - m_new)

    def half_step(j, u, carry):
        mxa, ma, mb, alpha_b = carry
        w = 1 - u
        mxb = mxu_step(j, qb, sb_ref.at[u], jnp.maximum(j - 1, 0), pb_ref.at[w], accb_ref, alpha_b)
        ma, alpha_a = softmax_step(sa_ref.at[u], pa_ref.at[u], mxa, ma)
        mxa = mxu_step(jnp.minimum(j + 1, n_k - 1), qa, sa_ref.at[w], j, pa_ref.at[u], acca_ref, alpha_a)
        mb, alpha_b = softmax_step(sb_ref.at[u], pb_ref.at[u], mxb, mb)
        return mxa, ma, mb, alpha_b

    mxa0 = scores(0, qa, sa_ref.at[0])
    pb_ref[1] = jnp.zeros((tk, tq), BF16)
    acca_ref[...] = jnp.zeros_like(acca_ref)
    accb_ref[...] = jnp.zeros_like(accb_ref)
    neg = jnp.full((1, tq), -jnp.inf, F32)
    _, _, _, alpha_b = lax.fori_loop(
        0, n_k // 2, lambda i, c: half_step(2 * i + 1, 1, half_step(2 * i, 0, c)),
        (mxa0, neg, neg, jnp.ones((1, tq), F32)), unroll=2)
    acc_a = acca_ref[...]
    acc_b = accb_ref[...] * alpha_b + _dot(vt_ref[n_k - 1], pb_ref[1])
    hd = DIFF_V_DIM
    la, lb = acc_a[hd:hd + 1], acc_b[hd:hd + 1]
    lam = (jnp.exp(jnp.sum(lq1_ref[...] * lk1_ref[...], axis=1, keepdims=True))
           - jnp.exp(jnp.sum(lq2_ref[...] * lk2_ref[...], axis=1, keepdims=True)) + lam_init)
    o = acc_a[:hd] * (1.0 / la) - acc_b[:hd] * (lam / lb)
    ms = jnp.mean(o * o, axis=0, keepdims=True)
    o = o * (lax.rsqrt(ms + SUBLN_EPS) * (1.0 - lam_init)) * g_ref[...]
    o_ref[...] = o.T.astype(BF16)


def _diff_attn(qa, qb, k, vt, lam_q1, lam_k1, lam_q2, lam_k2, subln_g, lam_init, tq):
    B, H, hd, S = qa.shape
    n_kb, vt_rows, tk = vt.shape[2:]
    assert n_kb % 2 == 0 and S % tq == 0
    q_spec = pl.BlockSpec((None, None, hd, tq), lambda b, h, i: (b, h, 0, i))
    lam_spec = _const_spec((1, DIFF_HEAD_DIM))
    return pl.pallas_call(
        functools.partial(_attn_kernel, tk=tk, lam_init=lam_init),
        grid=(B, H, S // tq),
        in_specs=[q_spec, q_spec,
                  pl.BlockSpec((None, None, S, hd), lambda b, h, i: (b, h, 0, 0)),
                  pl.BlockSpec((None, None, n_kb, vt_rows, tk), lambda b, h, i: (b, h, 0, 0, 0)),
                  lam_spec, lam_spec, lam_spec, lam_spec, _const_spec((hd, 1))],
        out_specs=pl.BlockSpec((None, tq, hd), lambda b, h, i: (b, i, h)),
        out_shape=jax.ShapeDtypeStruct((B, S, H * hd), BF16),
        scratch_shapes=[pltpu.VMEM((2, tk, tq), F32), pltpu.VMEM((2, tk, tq), F32),
                        pltpu.VMEM((2, tk, tq), BF16), pltpu.VMEM((2, tk, tq), BF16),
                        pltpu.VMEM((vt_rows, tq), F32), pltpu.VMEM((vt_rows, tq), F32)],
        compiler_params=_params(3),
        name="diff_attn",
    )(qa, qb, k, vt, lam_q1, lam_k1, lam_q2, lam_k2, subln_g)


def _ffn_kernel(*refs, tile, n_chunk, has_proj, final_norm):
    refs = list(refs)
    x_ref, xp_ref, xn_ref = refs[:3]
    del refs[:3]
    if has_proj:
        o_ref, op_ref, on_ref, wo_ref = refs[:4]
        del refs[:4]
    gn_ref, win_ref, cw_ref, cb_ref, wout_ref = refs[:5]
    del refs[:5]
    if final_norm:
        gf_ref = refs.pop(0)
    out_ref, y_ref = refs

    halo = SUBLANES
    i, n = pl.program_id(1), pl.num_programs(1)
    xa = jnp.concatenate([xp_ref[...], x_ref[...], xn_ref[...]], axis=0)
    if has_proj:
        oa = jnp.concatenate([op_ref[...], o_ref[...], on_ref[...]], axis=0)
        xa = xa + _dot(oa, wo_ref[...])
    h = _rms(xa, gn_ref[...], NORM_EPS)
    hb = jnp.concatenate([jnp.where(i > 0, h[:halo], 0.0), h[halo:halo + tile],
                          jnp.where(i < n - 1, h[halo + tile:], 0.0)], axis=0).astype(BF16)
    n_vr = tile // SUBLANES
    sub = lax.broadcasted_iota(jnp.int32, (1, SUBLANES, 1), 1)
    for j in range(n_chunk):
        a = _dot(hb, win_ref[j])
        cw = cw_ref[j]
        a3 = a.reshape(n_vr + 2, SUBLANES, 2 * FFN_CHUNK)
        dn = pltpu.roll(a3, 1, 1)
        up = pltpu.roll(a3, SUBLANES - 1, 1)
        a_prev = jnp.where(sub == 0, dn[0:n_vr], dn[1:n_vr + 1])
        a_next = jnp.where(sub == SUBLANES - 1, up[2:n_vr + 2], up[1:n_vr + 1])
        c = cb_ref[j] + a_prev * cw[0:1] + a3[1:n_vr + 1] * cw[1:2] + a_next * cw[2:3]
        c = c.reshape(tile, 2 * FFN_CHUNK)
        gate, lin = c[:, :FFN_CHUNK], c[:, FFN_CHUNK:]
        half_gate = 0.5 * gate
        y_ref[:, j * FFN_CHUNK:(j + 1) * FFN_CHUNK] = (
            half_gate * (1.0 + jnp.tanh(half_gate)) * lin).astype(BF16)
    out = xa[halo:halo + tile] + _dot(y_ref[...], wout_ref[...])
    if final_norm:
        out = _rms(out, gf_ref[...], NORM_EPS)
    out_ref[...] = out


def _conv_ffn(x, g_ffn, w_in, conv_w, conv_b, w_out, proj=None, g_final=None):
    B, S, D = x.shape
    T = min(ROW_TILE, S)
    halo = SUBLANES
    n_chunk = w_in.shape[0]
    assert S % T == 0 and T % halo == 0
    tb, last = T // halo, S // halo - 1

    def triple():
        return [pl.BlockSpec((None, T, D), lambda b, i: (b, i, 0)),
                pl.BlockSpec((None, halo, D), lambda b, i: (b, jnp.maximum(i * tb - 1, 0), 0)),
                pl.BlockSpec((None, halo, D), lambda b, i: (b, jnp.minimum((i + 1) * tb, last), 0))]

    args, specs = [x, x, x], triple()
    if proj is not None:
        o, w_o = proj
        args += [o, o, o, w_o]
        specs += triple() + [_const_spec(w_o.shape)]
    args += [g_ffn, w_in, conv_w, conv_b, w_out]
    specs += [_const_spec((1, D)), _const_spec(w_in.shape), _const_spec(conv_w.shape),
              _const_spec(conv_b.shape), _const_spec(w_out.shape)]
    if g_final is not None:
        args.append(g_final)
        specs.append(_const_spec((1, D)))
    return pl.pallas_call(
        functools.partial(_ffn_kernel, tile=T, n_chunk=n_chunk, has_proj=proj is not None,
                          final_norm=g_final is not None),
        grid=(B, S // T),
        in_specs=specs,
        out_specs=pl.BlockSpec((None, T, D), lambda b, i: (b, i, 0)),
        out_shape=jax.ShapeDtypeStruct((B, S, D), F32),
        scratch_shapes=[pltpu.VMEM((T, n_chunk * FFN_CHUNK), BF16)],
        compiler_params=_params(2),
        name="conv_ffn",
    )(*args)


def _rope_tables(S):
    half = DIFF_HEAD_DIM // 2
    pos = jnp.arange(S, dtype=F32)
    inv_freq = ROPE_THETA ** (-jnp.arange(0, DIFF_HEAD_DIM, 2, dtype=F32) / DIFF_HEAD_DIM)
    ang = pos[:, None] * inv_freq[None, :]
    cos, sin = jnp.cos(ang), jnp.sin(ang)
    cq = jnp.concatenate([cos, cos], axis=1).T
    sq = jnp.concatenate([sin, sin], axis=1).T
    ck = jnp.concatenate([cos] * 4, axis=1)
    sk = jnp.concatenate([-sin, -sin, sin, sin], axis=1)
    return cq, sq, ck, sk


def _head_feature_order():
    half = DIFF_HEAD_DIM // 2
    n = jnp.arange(DIFF_V_DIM)
    part, c, i = n // DIFF_HEAD_DIM, (n % DIFF_HEAD_DIM) // half, n % half
    within = c * DIFF_HEAD_DIM + part * half + i
    return (jnp.arange(DIFF_HEADS)[:, None] * DIFF_V_DIM + within[None, :]).reshape(-1)


def _prep_ffn(w_in, conv_w, conv_b, w_out):
    D, F2 = w_in.shape
    F = F2 // 2
    assert F % FFN_CHUNK == 0
    n = F // FFN_CHUNK

    def pair(a):
        g = a[..., :F].reshape(a.shape[:-1] + (n, FFN_CHUNK))
        u = a[..., F:].reshape(a.shape[:-1] + (n, FFN_CHUNK))
        return jnp.moveaxis(jnp.concatenate([g, u], axis=-1), -2, 0)

    return (pair(w_in).astype(BF16), pair(conv_w), pair(conv_b[None, :]),
            w_out.astype(BF16))


def _trunk(x, p):
    B, S, D = x.shape
    depth = p["norm_mix"].shape[0]
    tables = _rope_tables(S)
    pending = None
    for i in range(depth):
        j = i // 2
        g_mix = p["norm_mix"][i][None, :]
        if i % 2 == 0:
            x = _gmlp_layer(x, g_mix, *p["gmlp"][j])
        else:
            wqv_t, wk, w_o, lam, subln_g = p["diff"][j]
            qa, qb, k, vt = _qkv_rope(x, g_mix, wqv_t, wk, tables, min(K_TILE, S))
            lam_init = 0.8 - 0.6 * math.exp(-0.3 * i)
            o = _diff_attn(qa, qb, k, vt, *lam, subln_g, lam_init, min(Q_TILE, S))
            pending = (o, w_o)
        g_final = p["norm_final"][None, :] if i == depth - 1 else None
        x = _conv_ffn(x, p["norm_ffn"][i][None, :], *p["ffn"][i], proj=pending, g_final=g_final)
        pending = None
    return x


def kernel(x_prompt, x_sample, norm_mix, norm_ffn, norm_final, gmlp_w_in, gmlp_v_gain, gmlp_w_s, gmlp_b_s, gmlp_w_out, diff_w_qkv, diff_lam_q1, diff_lam_k1, diff_lam_q2, diff_lam_k2, diff_subln_g, diff_w_out, ffn_w_in, ffn_conv_w, ffn_conv_b, ffn_w_out):
    D = x_prompt.shape[-1]
    order = _head_feature_order()
    gmlp = []
    for j in range(gmlp_w_in.shape[0]):
        b_full = jnp.repeat(gmlp_b_s[j].T, CHUNK, axis=1)
        gmlp.append((gmlp_w_in[j].astype(BF16), gmlp_v_gain[j][None, :], gmlp_w_s[j].astype(BF16),
                     b_full, gmlp_w_out[j].astype(BF16)))
    diff = []
    for j in range(diff_w_qkv.shape[0]):
        w = diff_w_qkv[j]
        wq, wk, wv = w[:, :D][:, order], w[:, D:2 * D][:, order], w[:, 2 * D:]
        wqv_t = jnp.concatenate([wq, wv], axis=1).T.astype(BF16)
        lam = tuple(a[j][None, :] for a in (diff_lam_q1, diff_lam_k1, diff_lam_q2, diff_lam_k2))
        diff.append((wqv_t, wk.astype(BF16), diff_w_out[j].astype(BF16), lam,
                     diff_subln_g[j][:, None]))
    ffn = [_prep_ffn(ffn_w_in[i], ffn_conv_w[i], ffn_conv_b[i], ffn_w_out[i])
           for i in range(ffn_w_in.shape[0])]
    p = dict(norm_mix=norm_mix, norm_ffn=norm_ffn, norm_final=norm_final,
             gmlp=gmlp, diff=diff, ffn=ffn)
    return (_trunk(x_prompt, p), _trunk(x_sample, p))
```

```python
import functools
import math

import jax
import jax.numpy as jnp
from jax import lax
from jax.experimental import pallas as pl
from jax.experimental.pallas import tpu as pltpu

CHUNK = 128
GMLP_GROUPS = 8
DIFF_HEADS = 8
DIFF_HEAD_DIM = 64
DIFF_V_DIM = 2 * DIFF_HEAD_DIM
ROPE_THETA = 10000.0
NORM_EPS = 1e-6
SUBLN_EPS = 1e-5

SUBLANES = 8
MXU_WIDTH = 256
ROW_TILE = 512
GMLP_TILE = 1024
QKV_TILE = 1024
Q_TILE = 2048
K_TILE = 512
SOFTMAX_ROWS = 64
ROW_PAD = 128
SUM_ROWS = 16
FFN_CHUNK = MXU_WIDTH
VMEM_LIMIT = 56 * 1024 * 1024

F32 = jnp.float32
BF16 = jnp.bfloat16


def _dot(a, b):
    return jnp.dot(a, b, preferred_element_type=F32)


def _rms(x, g, eps):
    ms = jnp.mean(x * x, axis=-1, keepdims=True)
    return x * lax.rsqrt(ms + eps) * g


def _const_spec(shape):
    zeros = (0,) * len(shape)
    return pl.BlockSpec(shape, lambda *_: zeros, pipeline_mode=pl.Buffered(1))


def _params(n_grid):
    return pltpu.CompilerParams(
        dimension_semantics=("arbitrary",) * n_grid, vmem_limit_bytes=VMEM_LIMIT)


def _gmlp_kernel(x_ref, g_ref, win_ref, vg_ref, ws_ref, bs_ref, wout_ref, o_ref, y_ref,
                 *, n_chunks, width):
    x = x_ref[...]
    hb = _rms(x, g_ref[...], NORM_EPS).astype(BF16)
    z = _dot(hb, win_ref[...])
    z = 0.5 * z * (1.0 + jnp.tanh(math.sqrt(2.0 / math.pi) * (z + 0.044715 * (z * z * z))))
    u = z[:, :width]
    v = _rms(z[:, width:], vg_ref[...], NORM_EPS).astype(BF16)
    for g in range(GMLP_GROUPS):
        cols = slice(g * CHUNK, (g + 1) * CHUNK)
        vcat = jnp.concatenate(
            [v[c * CHUNK:(c + 1) * CHUNK, cols] for c in range(n_chunks)], axis=1)
        r = _dot(ws_ref[g], vcat)
        b = bs_ref[:, cols]
        for c in range(n_chunks):
            rows = slice(c * CHUNK, (c + 1) * CHUNK)
            y_ref[rows, cols] = (u[rows, cols] * (r[:, c * CHUNK:(c + 1) * CHUNK] + b)).astype(BF16)
    o_ref[...] = x + _dot(y_ref[...], wout_ref[...])


def _gmlp_layer(x, g_mix, w_in, v_gain, w_s, b_full, w_out):
    B, S, D = x.shape
    W = w_out.shape[0]
    T = min(GMLP_TILE, S)
    assert S % T == 0 and T % CHUNK == 0
    row = pl.BlockSpec((None, T, D), lambda b, i: (b, i, 0))
    return pl.pallas_call(
        functools.partial(_gmlp_kernel, n_chunks=T // CHUNK, width=W),
        grid=(B, S // T),
        in_specs=[row, _const_spec((1, D)), _const_spec((D, 2 * W)), _const_spec((1, W)),
                  _const_spec((GMLP_GROUPS, CHUNK, CHUNK)), _const_spec((CHUNK, W)),
                  _const_spec((W, D))],
        out_specs=row,
        out_shape=jax.ShapeDtypeStruct((B, S, D), F32),
        scratch_shapes=[pltpu.VMEM((T, W), BF16)],
        compiler_params=_params(2),
        name="gmlp_layer",
    )(x, g_mix, w_in, v_gain, w_s, b_full, w_out)


def _qkv_kernel(x_ref, g_ref, wqv_ref, wk_ref, cq_ref, sq_ref, ck_ref, sk_ref,
                qa_ref, qb_ref, k_ref, vt_ref, *, d_model, tk):
    hb = _rms(x_ref[...], g_ref[...], NORM_EPS).astype(BF16)
    qv_t = lax.dot_general(wqv_ref[...], hb, (((1,), (1,)), ((), ())),
                           preferred_element_type=F32)
    kk = _dot(hb, wk_ref[...])
    cq, sq = cq_ref[...], sq_ref[...]
    ck, sk = ck_ref[...], sk_ref[...]
    scale = DIFF_HEAD_DIM ** -0.5 * math.log2(math.e)
    half = DIFF_HEAD_DIM // 2
    zero = jnp.zeros((half, hb.shape[0]), F32)
    ones = jnp.ones((SUM_ROWS, hb.shape[0]), F32)
    for h in range(DIFF_HEADS):
        xh = qv_t[h * DIFF_V_DIM:(h + 1) * DIFF_V_DIM]
        x1, x2 = xh[:DIFF_HEAD_DIM], xh[DIFF_HEAD_DIM:]
        o1 = (x1 * cq - x2 * sq) * scale
        o2 = (x2 * cq + x1 * sq) * scale
        qa_ref[h] = jnp.concatenate([o1[:half], zero, o2[:half], zero], axis=0).astype(BF16)
        qb_ref[h] = jnp.concatenate([zero, o1[half:], zero, o2[half:]], axis=0).astype(BF16)
        vt = jnp.concatenate(
            [qv_t[d_model + h * DIFF_V_DIM:d_model + (h + 1) * DIFF_V_DIM], ones], axis=0).astype(BF16)
        for kb in range(vt_ref.shape[1]):
            vt_ref[h, kb] = vt[:, kb * tk:(kb + 1) * tk]
        kh = kk[:, h * DIFF_V_DIM:(h + 1) * DIFF_V_DIM]
        k_ref[h] = (kh * ck + pltpu.roll(kh, DIFF_HEAD_DIM, 1) * sk).astype(BF16)


def _qkv_rope(x, g_mix, wqv_t, wk, tables, tk):
    B, S, D = x.shape
    T = min(QKV_TILE, S)
    assert S % T == 0 and T % tk == 0
    H = DIFF_HEADS
    cq, sq, ck, sk = tables
    hd = DIFF_V_DIM
    q_spec = pl.BlockSpec((None, H, hd, T), lambda b, i: (b, 0, 0, i))
    return pl.pallas_call(
        functools.partial(_qkv_kernel, d_model=D, tk=tk),
        grid=(B, S // T),
        in_specs=[pl.BlockSpec((None, T, D), lambda b, i: (b, i, 0)),
                  _const_spec((1, D)), _const_spec((2 * D, D)), _const_spec((D, D)),
                  pl.BlockSpec((DIFF_HEAD_DIM, T), lambda b, i: (0, i)),
                  pl.BlockSpec((DIFF_HEAD_DIM, T), lambda b, i: (0, i)),
                  pl.BlockSpec((T, hd), lambda b, i: (i, 0)),
                  pl.BlockSpec((T, hd), lambda b, i: (i, 0))],
        out_specs=[q_spec, q_spec,
                   pl.BlockSpec((None, H, T, hd), lambda b, i: (b, 0, i, 0)),
                   pl.BlockSpec((None, H, T // tk, hd + SUM_ROWS, tk), lambda b, i: (b, 0, i, 0, 0))],
        out_shape=[jax.ShapeDtypeStruct((B, H, hd, S), BF16),
                   jax.ShapeDtypeStruct((B, H, hd, S), BF16),
                   jax.ShapeDtypeStruct((B, H, S, hd), BF16),
                   jax.ShapeDtypeStruct((B, H, S // tk, hd + SUM_ROWS, tk), BF16)],
        compiler_params=_params(2),
        name="qkv_rope",
    )(x, g_mix, wqv_t, wk, cq, sq, ck, sk)


def _attn_kernel(qa_ref, qb_ref, k_ref, vt_ref, lq1_ref, lk1_ref, lq2_ref, lk2_ref, g_ref,
                 o_ref, sa_ref, sb_ref, pa_ref, pb_ref, acca_ref, accb_ref, *, tk, lam_init):
    qa, qb = qa_ref[...], qb_ref[...]
    tq = qa.shape[1]
    n_k = k_ref.shape[0] // tk

    def k_block(i):
        return k_ref[pl.ds(pl.multiple_of(i * tk, tk), tk), :]

    def col_max(s):
        part = jnp.max(s.reshape(s.shape[0] // SUBLANES, SUBLANES, s.shape[1]), axis=0)
        return jnp.max(part, axis=0, keepdims=True)

    def scores(i, q, s_ref):
        s = _dot(k_block(i), q)
        s_ref[:, :tq] = s
        return col_max(s)

    def mxu_step(k_idx, q, s_ref, v_idx, p_ref, acc_ref, alpha):
        k, vt = k_block(k_idx), vt_ref[v_idx]
        mx = []
        for c in range(0, tq, MXU_WIDTH):
            cols = slice(c, c + MXU_WIDTH)
            s = _dot(k, q[:, cols])
            s_ref[:, cols] = s
            mx.append(col_max(s))
            acc_ref[:, cols] = acc_ref[:, cols] * alpha[:, cols] + _dot(vt, p_ref[:, cols])
        return jnp.concatenate(mx, axis=1)

    def softmax_step(s_ref, p_ref, m_blk, m):
        m_new = jnp.maximum(m, m_blk)
        for r in range(0, tk, SOFTMAX_ROWS):
            p_ref[r:r + SOFTMAX_ROWS, :tq] = jnp.exp2(s_ref[r:r + SOFTMAX_ROWS, :tq] - m_new).astype(BF16)
        return m_new, jnp.exp2(m - m_new)

    def half_step(j, u, carry):
        mxa, ma, mb, alpha_b = carry
        w = 1 - u
        mxb = mxu_step(j, qb, sb_ref.at[u], jnp.maximum(j - 1, 0), pb_ref.at[w], accb_ref, alpha_b)
        ma, alpha_a = softmax_step(sa_ref.at[u], pa_ref.at[u], mxa, ma)
        mxa = mxu_step(jnp.minimum(j + 1, n_k - 1), qa, sa_ref.at[w], j, pa_ref.at[u], acca_ref, alpha_a)
        mb, alpha_b = softmax_step(sb_ref.at[u], pb_ref.at[u], mxb, mb)
        return mxa, ma, mb, alpha_b

    mxa0 = scores(0, qa, sa_ref.at[0])
    pb_ref[1, :, :tq] = jnp.zeros((tk, tq), BF16)
    acca_ref[...] = jnp.zeros_like(acca_ref)
    accb_ref[...] = jnp.zeros_like(accb_ref)
    neg = jnp.full((1, tq), -jnp.inf, F32)
    _, _, _, alpha_b = lax.fori_loop(
        0, n_k // 2, lambda i, c: half_step(2 * i + 1, 1, half_step(2 * i, 0, c)),
        (mxa0, neg, neg, jnp.ones((1, tq), F32)), unroll=2)
    acc_a = acca_ref[...]
    acc_b = accb_ref[...] * alpha_b + _dot(vt_ref[n_k - 1], pb_ref[1, :, :tq])
    hd = DIFF_V_DIM
    la, lb = acc_a[hd:hd + 1], acc_b[hd:hd + 1]
    lam = (jnp.exp(jnp.sum(lq1_ref[...] * lk1_ref[...], axis=1, keepdims=True))
           - jnp.exp(jnp.sum(lq2_ref[...] * lk2_ref[...], axis=1, keepdims=True)) + lam_init)
    o = acc_a[:hd] * (1.0 / la) - acc_b[:hd] * (lam / lb)
    ms = jnp.mean(o * o, axis=0, keepdims=True)
    o = o * (lax.rsqrt(ms + SUBLN_EPS) * (1.0 - lam_init)) * g_ref[...]
    o_ref[...] = o.T.astype(BF16)


def _diff_attn(qa, qb, k, vt, lam_q1, lam_k1, lam_q2, lam_k2, subln_g, lam_init, tq):
    B, H, hd, S = qa.shape
    n_kb, vt_rows, tk = vt.shape[2:]
    assert n_kb % 2 == 0 and S % tq == 0
    q_spec = pl.BlockSpec((None, None, hd, tq), lambda b, h, i: (b, h, 0, i))
    lam_spec = _const_spec((1, DIFF_HEAD_DIM))
    return pl.pallas_call(
        functools.partial(_attn_kernel, tk=tk, lam_init=lam_init),
        grid=(B, H, S // tq),
        in_specs=[q_spec, q_spec,
                  pl.BlockSpec((None, None, S, hd), lambda b, h, i: (b, h, 0, 0)),
                  pl.BlockSpec((None, None, n_kb, vt_rows, tk), lambda b, h, i: (b, h, 0, 0, 0)),
                  lam_spec, lam_spec, lam_spec, lam_spec, _const_spec((hd, 1))],
        out_specs=pl.BlockSpec((None, tq, hd), lambda b, h, i: (b, i, h)),
        out_shape=jax.ShapeDtypeStruct((B, S, H * hd), BF16),
        scratch_shapes=[pltpu.VMEM((2, tk, tq + ROW_PAD), F32), pltpu.VMEM((2, tk, tq + ROW_PAD), F32),
                        pltpu.VMEM((2, tk, tq + ROW_PAD), BF16), pltpu.VMEM((2, tk, tq + ROW_PAD), BF16),
                        pltpu.VMEM((vt_rows, tq), F32), pltpu.VMEM((vt_rows, tq), F32)],
        compiler_params=_params(3),
        name="diff_attn",
    )(qa, qb, k, vt, lam_q1, lam_k1, lam_q2, lam_k2, subln_g)


def _ffn_kernel(*refs, tile, n_chunk, has_proj, final_norm):
    refs = list(refs)
    x_ref, xp_ref, xn_ref = refs[:3]
    del refs[:3]
    if has_proj:
        o_ref, op_ref, on_ref, wo_ref = refs[:4]
        del refs[:4]
    gn_ref, win_ref, cw_ref, cb_ref, wout_ref = refs[:5]
    del refs[:5]
    if final_norm:
        gf_ref = refs.pop(0)
    out_ref, y_ref = refs

    halo = SUBLANES
    i, n = pl.program_id(1), pl.num_programs(1)
    xa = jnp.concatenate([xp_ref[...], x_ref[...], xn_ref[...]], axis=0)
    if has_proj:
        oa = jnp.concatenate([op_ref[...], o_ref[...], on_ref[...]], axis=0)
        xa = xa + _dot(oa, wo_ref[...])
    h = _rms(xa, gn_ref[...], NORM_EPS)
    hb = jnp.concatenate([jnp.where(i > 0, h[:halo], 0.0), h[halo:halo + tile],
                          jnp.where(i < n - 1, h[halo + tile:], 0.0)], axis=0).astype(BF16)
    n_vr = tile // SUBLANES
    sub = lax.broadcasted_iota(jnp.int32, (1, SUBLANES, 1), 1)
    for j in range(n_chunk):
        a = _dot(hb, win_ref[j])
        cw = cw_ref[j]
        a3 = a.reshape(n_vr + 2, SUBLANES, 2 * FFN_CHUNK)
        dn = pltpu.roll(a3, 1, 1)
        up = pltpu.roll(a3, SUBLANES - 1, 1)
        a_prev = jnp.where(sub == 0, dn[0:n_vr], dn[1:n_vr + 1])
        a_next = jnp.where(sub == SUBLANES - 1, up[2:n_vr + 2], up[1:n_vr + 1])
        c = cb_ref[j] + a_prev * cw[0:1] + a3[1:n_vr + 1] * cw[1:2] + a_next * cw[2:3]
        c = c.reshape(tile, 2 * FFN_CHUNK)
        gate, lin = c[:, :FFN_CHUNK], c[:, FFN_CHUNK:]
        half_gate = 0.5 * gate
        y_ref[:, j * FFN_CHUNK:(j + 1) * FFN_CHUNK] = (
            half_gate * (1.0 + jnp.tanh(half_gate)) * lin).astype(BF16)
    out = xa[halo:halo + tile] + _dot(y_ref[...], wout_ref[...])
    if final_norm:
        out = _rms(out, gf_ref[...], NORM_EPS)
    out_ref[...] = out


def _conv_ffn(x, g_ffn, w_in, conv_w, conv_b, w_out, proj=None, g_final=None):
    B, S, D = x.shape
    T = min(ROW_TILE, S)
    halo = SUBLANES
    n_chunk = w_in.shape[0]
    assert S % T == 0 and T % halo == 0
    tb, last = T // halo, S // halo - 1

    def triple():
        return [pl.BlockSpec((None, T, D), lambda b, i: (b, i, 0)),
                pl.BlockSpec((None, halo, D), lambda b, i: (b, jnp.maximum(i * tb - 1, 0), 0)),
                pl.BlockSpec((None, halo, D), lambda b, i: (b, jnp.minimum((i + 1) * tb, last), 0))]

    args, specs = [x, x, x], triple()
    if proj is not None:
        o, w_o = proj
        args += [o, o, o, w_o]
        specs += triple() + [_const_spec(w_o.shape)]
    args += [g_ffn, w_in, conv_w, conv_b, w_out]
    specs += [_const_spec((1, D)), _const_spec(w_in.shape), _const_spec(conv_w.shape),
              _const_spec(conv_b.shape), _const_spec(w_out.shape)]
    if g_final is not None:
        args.append(g_final)
        specs.append(_const_spec((1, D)))
    return pl.pallas_call(
        functools.partial(_ffn_kernel, tile=T, n_chunk=n_chunk, has_proj=proj is not None,
                          final_norm=g_final is not None),
        grid=(B, S // T),
        in_specs=specs,
        out_specs=pl.BlockSpec((None, T, D), lambda b, i: (b, i, 0)),
        out_shape=jax.ShapeDtypeStruct((B, S, D), F32),
        scratch_shapes=[pltpu.VMEM((T, n_chunk * FFN_CHUNK), BF16)],
        compiler_params=_params(2),
        name="conv_ffn",
    )(*args)


def _rope_tables(S):
    half = DIFF_HEAD_DIM // 2
    pos = jnp.arange(S, dtype=F32)
    inv_freq = ROPE_THETA ** (-jnp.arange(0, DIFF_HEAD_DIM, 2, dtype=F32) / DIFF_HEAD_DIM)
    ang = pos[:, None] * inv_freq[None, :]
    cos, sin = jnp.cos(ang), jnp.sin(ang)
    cq = jnp.concatenate([cos, cos], axis=1).T
    sq = jnp.concatenate([sin, sin], axis=1).T
    ck = jnp.concatenate([cos] * 4, axis=1)
    sk = jnp.concatenate([-sin, -sin, sin, sin], axis=1)
    return cq, sq, ck, sk


def _head_feature_order():
    half = DIFF_HEAD_DIM // 2
    n = jnp.arange(DIFF_V_DIM)
    part, c, i = n // DIFF_HEAD_DIM, (n % DIFF_HEAD_DIM) // half, n % half
    within = c * DIFF_HEAD_DIM + part * half + i
    return (jnp.arange(DIFF_HEADS)[:, None] * DIFF_V_DIM + within[None, :]).reshape(-1)


def _prep_ffn(w_in, conv_w, conv_b, w_out):
    D, F2 = w_in.shape
    F = F2 // 2
    assert F % FFN_CHUNK == 0
    n = F // FFN_CHUNK

    def pair(a):
        g = a[..., :F].reshape(a.shape[:-1] + (n, FFN_CHUNK))
        u = a[..., F:].reshape(a.shape[:-1] + (n, FFN_CHUNK))
        return jnp.moveaxis(jnp.concatenate([g, u], axis=-1), -2, 0)

    return (pair(w_in).astype(BF16), pair(conv_w), pair(conv_b[None, :]),
            w_out.astype(BF16))


def _trunk(x, p):
    B, S, D = x.shape
    depth = p["norm_mix"].shape[0]
    tables = _rope_tables(S)
    pending = None
    for i in range(depth):
        j = i // 2
        g_mix = p["norm_mix"][i][None, :]
        if i % 2 == 0:
            x = _gmlp_layer(x, g_mix, *p["gmlp"][j])
        else:
            wqv_t, wk, w_o, lam, subln_g = p["diff"][j]
            qa, qb, k, vt = _qkv_rope(x, g_mix, wqv_t, wk, tables, min(K_TILE, S))
            lam_init = 0.8 - 0.6 * math.exp(-0.3 * i)
            o = _diff_attn(qa, qb, k, vt, *lam, subln_g, lam_init, min(Q_TILE, S))
            pending = (o, w_o)
        g_final = p["norm_final"][None, :] if i == depth - 1 else None
        x = _conv_ffn(x, p["norm_ffn"][i][None, :], *p["ffn"][i], proj=pending, g_final=g_final)
        pending = None
    return x


def kernel(x_prompt, x_sample, norm_mix, norm_ffn, norm_final, gmlp_w_in, gmlp_v_gain, gmlp_w_s, gmlp_b_s, gmlp_w_out, diff_w_qkv, diff_lam_q1, diff_lam_k1, diff_lam_q2, diff_lam_k2, diff_subln_g, diff_w_out, ffn_w_in, ffn_conv_w, ffn_conv_b, ffn_w_out):
    D = x_prompt.shape[-1]
    order = _head_feature_order()
    gmlp = []
    for j in range(gmlp_w_in.shape[0]):
        b_full = jnp.repeat(gmlp_b_s[j].T, CHUNK, axis=1)
        gmlp.append((gmlp_w_in[j].astype(BF16), gmlp_v_gain[j][None, :], gmlp_w_s[j].astype(BF16),
                     b_full, gmlp_w_out[j].astype(BF16)))
    diff = []
    for j in range(diff_w_qkv.shape[0]):
        w = diff_w_qkv[j]
        wq, wk, wv = w[:, :D][:, order], w[:, D:2 * D][:, order], w[:, 2 * D:]
        wqv_t = jnp.concatenate([wq, wv], axis=1).T.astype(BF16)
        lam = tuple(a[j][None, :] for a in (diff_lam_q1, diff_lam_k1, diff_lam_q2, diff_lam_k2))
        diff.append((wqv_t, wk.astype(BF16), diff_w_out[j].astype(BF16), lam,
                     diff_subln_g[j][:, None]))
    ffn = [_prep_ffn(ffn_w_in[i], ffn_conv_w[i], ffn_conv_b[i], ffn_w_out[i])
           for i in range(ffn_w_in.shape[0])]
    p = dict(norm_mix=norm_mix, norm_ffn=norm_ffn, norm_final=norm_final,
             gmlp=gmlp, diff=diff, ffn=ffn)
    return (_trunk(x_prompt, p), _trunk(x_sample, p))
```

```python
import functools
import math

import jax
import jax.numpy as jnp
from jax import lax
from jax.experimental import pallas as pl
from jax.experimental.pallas import tpu as pltpu

CHUNK = 128
GMLP_GROUPS = 8
DIFF_HEADS = 8
DIFF_HEAD_DIM = 64
DIFF_V_DIM = 2 * DIFF_HEAD_DIM
ROPE_THETA = 10000.0
NORM_EPS = 1e-6
SUBLN_EPS = 1e-5

SUBLANES = 8
MXU_WIDTH = 256
ROW_TILE = 512
GMLP_TILE = 1024
QKV_TILE = 1024
Q_TILE = 2048
K_TILE = 512
SOFTMAX_ROWS = 64
SUM_ROWS = 16
FFN_CHUNK = MXU_WIDTH
VMEM_LIMIT = 56 * 1024 * 1024

F32 = jnp.float32
BF16 = jnp.bfloat16


def _dot(a, b):
    return jnp.dot(a, b, preferred_element_type=F32)


def _rms(x, g, eps):
    ms = jnp.mean(x * x, axis=-1, keepdims=True)
    return x * lax.rsqrt(ms + eps) * g


def _const_spec(shape):
    zeros = (0,) * len(shape)
    return pl.BlockSpec(shape, lambda *_: zeros, pipeline_mode=pl.Buffered(1))


def _params(n_grid):
    return pltpu.CompilerParams(
        dimension_semantics=("arbitrary",) * n_grid, vmem_limit_bytes=VMEM_LIMIT)


def _gmlp_kernel(x_ref, g_ref, win_ref, vg_ref, ws_ref, bs_ref, wout_ref, o_ref, y_ref,
                 *, n_chunks, width):
    x = x_ref[...]
    hb = _rms(x, g_ref[...], NORM_EPS).astype(BF16)
    z = _dot(hb, win_ref[...])
    z = 0.5 * z * (1.0 + jnp.tanh(math.sqrt(2.0 / math.pi) * (z + 0.044715 * (z * z * z))))
    u = z[:, :width]
    v = _rms(z[:, width:], vg_ref[...], NORM_EPS).astype(BF16)
    for g in range(GMLP_GROUPS):
        cols = slice(g * CHUNK, (g + 1) * CHUNK)
        vcat = jnp.concatenate(
            [v[c * CHUNK:(c + 1) * CHUNK, cols] for c in range(n_chunks)], axis=1)
        r = _dot(ws_ref[g], vcat)
        b = bs_ref[:, cols]
        for c in range(n_chunks):
            rows = slice(c * CHUNK, (c + 1) * CHUNK)
            y_ref[rows, cols] = (u[rows, cols] * (r[:, c * CHUNK:(c + 1) * CHUNK] + b)).astype(BF16)
    o_ref[...] = x + _dot(y_ref[...], wout_ref[...])


def _gmlp_layer(x, g_mix, w_in, v_gain, w_s, b_full, w_out):
    B, S, D = x.shape
    W = w_out.shape[0]
    T = min(GMLP_TILE, S)
    assert S % T == 0 and T % CHUNK == 0
    row = pl.BlockSpec((None, T, D), lambda b, i: (b, i, 0))
    return pl.pallas_call(
        functools.partial(_gmlp_kernel, n_chunks=T // CHUNK, width=W),
        grid=(B, S // T),
        in_specs=[row, _const_spec((1, D)), _const_spec((D, 2 * W)), _const_spec((1, W)),
                  _const_spec((GMLP_GROUPS, CHUNK, CHUNK)), _const_spec((CHUNK, W)),
                  _const_spec((W, D))],
        out_specs=row,
        out_shape=jax.ShapeDtypeStruct((B, S, D), F32),
        scratch_shapes=[pltpu.VMEM((T, W), BF16)],
        compiler_params=_params(2),
        name="gmlp_layer",
    )(x, g_mix, w_in, v_gain, w_s, b_full, w_out)


def _qkv_kernel(x_ref, g_ref, wqv_ref, wk_ref, cq_ref, sq_ref, ck_ref, sk_ref,
                qa_ref, qb_ref, k_ref, vt_ref, *, d_model, tk):
    hb = _rms(x_ref[...], g_ref[...], NORM_EPS).astype(BF16)
    qv_t = lax.dot_general(wqv_ref[...], hb, (((1,), (1,)), ((), ())),
                           preferred_element_type=F32)
    kk = _dot(hb, wk_ref[...])
    cq, sq = cq_ref[...], sq_ref[...]
    ck, sk = ck_ref[...], sk_ref[...]
    scale = DIFF_HEAD_DIM ** -0.5 * math.log2(math.e)
    half = DIFF_HEAD_DIM // 2
    zero = jnp.zeros((half, hb.shape[0]), F32)
    ones = jnp.ones((SUM_ROWS, hb.shape[0]), F32)
    for h in range(DIFF_HEADS):
        xh = qv_t[h * DIFF_V_DIM:(h + 1) * DIFF_V_DIM]
        x1, x2 = xh[:DIFF_HEAD_DIM], xh[DIFF_HEAD_DIM:]
        o1 = (x1 * cq - x2 * sq) * scale
        o2 = (x2 * cq + x1 * sq) * scale
        qa_ref[h] = jnp.concatenate([o1[:half], zero, o2[:half], zero], axis=0).astype(BF16)
        qb_ref[h] = jnp.concatenate([zero, o1[half:], zero, o2[half:]], axis=0).astype(BF16)
        vt = jnp.concatenate(
            [qv_t[d_model + h * DIFF_V_DIM:d_model + (h + 1) * DIFF_V_DIM], ones], axis=0).astype(BF16)
        for kb in range(vt_ref.shape[1]):
            vt_ref[h, kb] = vt[:, kb * tk:(kb + 1) * tk]
        kh = kk[:, h * DIFF_V_DIM:(h + 1) * DIFF_V_DIM]
        k_ref[h] = (kh * ck + pltpu.roll(kh, DIFF_HEAD_DIM, 1) * sk).astype(BF16)


def _qkv_rope(x, g_mix, wqv_t, wk, tables, tk):
    B, S, D = x.shape
    T = min(QKV_TILE, S)
    assert S % T == 0 and T % tk == 0
    H = DIFF_HEADS
    cq, sq, ck, sk = tables
    hd = DIFF_V_DIM
    q_spec = pl.BlockSpec((None, H, hd, T), lambda b, i: (b, 0, 0, i))
    return pl.pallas_call(
        functools.partial(_qkv_kernel, d_model=D, tk=tk),
        grid=(B, S // T),
        in_specs=[pl.BlockSpec((None, T, D), lambda b, i: (b, i, 0)),
                  _const_spec((1, D)), _const_spec((2 * D, D)), _const_spec((D, D)),
                  pl.BlockSpec((DIFF_HEAD_DIM, T), lambda b, i: (0, i)),
                  pl.BlockSpec((DIFF_HEAD_DIM, T), lambda b, i: (0, i)),
                  pl.BlockSpec((T, hd), lambda b, i: (i, 0)),
                  pl.BlockSpec((T, hd), lambda b, i: (i, 0))],
        out_specs=[q_spec, q_spec,
                   pl.BlockSpec((None, H, T, hd), lambda b, i: (b, 0, i, 0)),
                   pl.BlockSpec((None, H, T // tk, hd + SUM_ROWS, tk), lambda b, i: (b, 0, i, 0, 0))],
        out_shape=[jax.ShapeDtypeStruct((B, H, hd, S), BF16),
                   jax.ShapeDtypeStruct((B, H, hd, S), BF16),
                   jax.ShapeDtypeStruct((B, H, S, hd), BF16),
                   jax.ShapeDtypeStruct((B, H, S // tk, hd + SUM_ROWS, tk), BF16)],
        compiler_params=_params(2),
        name="qkv_rope",
    )(x, g_mix, wqv_t, wk, cq, sq, ck, sk)


def _attn_kernel(qa_ref, qb_ref, k_ref, vt_ref, lq1_ref, lk1_ref, lq2_ref, lk2_ref, g_ref,
                 o_ref, sa_ref, sb_ref, pa_ref, pb_ref, acca_ref, accb_ref, *, tk, lam_init):
    qa, qb = qa_ref, qb_ref
    tq = qa.shape[1]
    n_k = k_ref.shape[0] // tk

    def k_block(i):
        return k_ref[pl.ds(pl.multiple_of(i * tk, tk), tk), :]

    def col_max(s):
        part = jnp.max(s.reshape(s.shape[0] // SUBLANES, SUBLANES, s.shape[1]), axis=0)
        return jnp.max(part, axis=0, keepdims=True)

    def scores(i, q, s_ref):
        s = _dot(k_block(i), q[...])
        s_ref[...] = s
        return col_max(s)

    def mxu_step(k_idx, q, s_ref, v_idx, p_ref, acc_ref, alpha):
        k, vt = k_block(k_idx), vt_ref[v_idx]
        mx = []
        for c in range(0, tq, MXU_WIDTH):
            cols = slice(c, c + MXU_WIDTH)
            s = _dot(k, q[:, cols])
            s_ref[:, cols] = s
            mx.append(col_max(s))
            acc_ref[:, cols] = acc_ref[:, cols] * alpha[:, cols] + _dot(vt, p_ref[:, cols])
        return jnp.concatenate(mx, axis=1)

    def softmax_step(s_ref, p_ref, m_blk, m):
        m_new = jnp.maximum(m, m_blk)
        for r in range(0, tk, SOFTMAX_ROWS):
            p_ref[r:r + SOFTMAX_ROWS, :] = jnp.exp2(s_ref[r:r + SOFTMAX_ROWS, :] - m_new).astype(BF16)
        return m_new, jnp.exp2(m - m_new)

    def half_step(j, u, carry):
        mxa, ma, mb, alpha_b = carry
        w = 1 - u
        mxb = mxu_step(j, qb, sb_ref.at[u], jnp.maximum(j - 1, 0), pb_ref.at[w], accb_ref, alpha_b)
        ma, alpha_a = softmax_step(sa_ref.at[u], pa_ref.at[u], mxa, ma)
        mxa = mxu_step(jnp.minimum(j + 1, n_k - 1), qa, sa_ref.at[w], j, pa_ref.at[u], acca_ref, alpha_a)
        mb, alpha_b = softmax_step(sb_ref.at[u], pb_ref.at[u], mxb, mb)
        return mxa, ma, mb, alpha_b

    mxa0 = scores(0, qa, sa_ref.at[0])
    pb_ref[1] = jnp.zeros((tk, tq), BF16)
    acca_ref[...] = jnp.zeros_like(acca_ref)
    accb_ref[...] = jnp.zeros_like(accb_ref)
    neg = jnp.full((1, tq), -jnp.inf, F32)
    _, _, _, alpha_b = lax.fori_loop(
        0, n_k // 2, lambda i, c: half_step(2 * i + 1, 1, half_step(2 * i, 0, c)),
        (mxa0, neg, neg, jnp.ones((1, tq), F32)), unroll=2)
    acc_a = acca_ref[...]
    acc_b = accb_ref[...] * alpha_b + _dot(vt_ref[n_k - 1], pb_ref[1])
    hd = DIFF_V_DIM
    la, lb = acc_a[hd:hd + 1], acc_b[hd:hd + 1]
    lam = (jnp.exp(jnp.sum(lq1_ref[...] * lk1_ref[...], axis=1, keepdims=True))
           - jnp.exp(jnp.sum(lq2_ref[...] * lk2_ref[...], axis=1, keepdims=True)) + lam_init)
    o = acc_a[:hd] * (1.0 / la) - acc_b[:hd] * (lam / lb)
    ms = jnp.mean(o * o, axis=0, keepdims=True)
    o = o * (lax.rsqrt(ms + SUBLN_EPS) * (1.0 - lam_init)) * g_ref[...]
    o_ref[...] = o.T.astype(BF16)


def _diff_attn(qa, qb, k, vt, lam_q1, lam_k1, lam_q2, lam_k2, subln_g, lam_init, tq):
    B, H, hd, S = qa.shape
    n_kb, vt_rows, tk = vt.shape[2:]
    assert n_kb % 2 == 0 and S % tq == 0
    q_spec = pl.BlockSpec((None, None, hd, tq), lambda b, h, i: (b, h, 0, i))
    lam_spec = _const_spec((1, DIFF_HEAD_DIM))
    return pl.pallas_call(
        functools.partial(_attn_kernel, tk=tk, lam_init=lam_init),
        grid=(B, H, S // tq),
        in_specs=[q_spec, q_spec,
                  pl.BlockSpec((None, None, S, hd), lambda b, h, i: (b, h, 0, 0)),
                  pl.BlockSpec((None, None, n_kb, vt_rows, tk), lambda b, h, i: (b, h, 0, 0, 0)),
                  lam_spec, lam_spec, lam_spec, lam_spec, _const_spec((hd, 1))],
        out_specs=pl.BlockSpec((None, tq, hd), lambda b, h, i: (b, i, h)),
        out_shape=jax.ShapeDtypeStruct((B, S, H * hd), BF16),
        scratch_shapes=[pltpu.VMEM((2, tk, tq), F32), pltpu.VMEM((2, tk, tq), F32),
                        pltpu.VMEM((2, tk, tq), BF16), pltpu.VMEM((2, tk, tq), BF16),
                        pltpu.VMEM((vt_rows, tq), F32), pltpu.VMEM((vt_rows, tq), F32)],
        compiler_params=_params(3),
        name="diff_attn",
    )(qa, qb, k, vt, lam_q1, lam_k1, lam_q2, lam_k2, subln_g)


def _ffn_kernel(*refs, tile, n_chunk, has_proj, final_norm):
    refs = list(refs)
    x_ref, xp_ref, xn_ref = refs[:3]
    del refs[:3]
    if has_proj:
        o_ref, op_ref, on_ref, wo_ref = refs[:4]
        del refs[:4]
    gn_ref, win_ref, cw_ref, cb_ref, wout_ref = refs[:5]
    del refs[:5]
    if final_norm:
        gf_ref = refs.pop(0)
    out_ref, y_ref = refs

    halo = SUBLANES
    i, n = pl.program_id(1), pl.num_programs(1)
    xa = jnp.concatenate([xp_ref[...], x_ref[...], xn_ref[...]], axis=0)
    if has_proj:
        oa = jnp.concatenate([op_ref[...], o_ref[...], on_ref[...]], axis=0)
        xa = xa + _dot(oa, wo_ref[...])
    h = _rms(xa, gn_ref[...], NORM_EPS)
    hb = jnp.concatenate([jnp.where(i > 0, h[:halo], 0.0), h[halo:halo + tile],
                          jnp.where(i < n - 1, h[halo + tile:], 0.0)], axis=0).astype(BF16)
    n_vr = tile // SUBLANES
    sub = lax.broadcasted_iota(jnp.int32, (1, SUBLANES, 1), 1)
    for j in range(n_chunk):
        a = _dot(hb, win_ref[j])
        cw = cw_ref[j]
        a3 = a.reshape(n_vr + 2, SUBLANES, 2 * FFN_CHUNK)
        dn = pltpu.roll(a3, 1, 1)
        up = pltpu.roll(a3, SUBLANES - 1, 1)
        a_prev = jnp.where(sub == 0, dn[0:n_vr], dn[1:n_vr + 1])
        a_next = jnp.where(sub == SUBLANES - 1, up[2:n_vr + 2], up[1:n_vr + 1])
        c = cb_ref[j] + a_prev * cw[0:1] + a3[1:n_vr + 1] * cw[1:2] + a_next * cw[2:3]
        c = c.reshape(tile, 2 * FFN_CHUNK)
        gate, lin = c[:, :FFN_CHUNK], c[:, FFN_CHUNK:]
        half_gate = 0.5 * gate
        y_ref[:, j * FFN_CHUNK:(j + 1) * FFN_CHUNK] = (
            half_gate * (1.0 + jnp.tanh(half_gate)) * lin).astype(BF16)
    out = xa[halo:halo + tile] + _dot(y_ref[...], wout_ref[...])
    if final_norm:
        out = _rms(out, gf_ref[...], NORM_EPS)
    out_ref[...] = out


def _conv_ffn(x, g_ffn, w_in, conv_w, conv_b, w_out, proj=None, g_final=None):
    B, S, D = x.shape
    T = min(ROW_TILE, S)
    halo = SUBLANES
    n_chunk = w_in.shape[0]
    assert S % T == 0 and T % halo == 0
    tb, last = T // halo, S // halo - 1

    def triple():
        return [pl.BlockSpec((None, T, D), lambda b, i: (b, i, 0)),
                pl.BlockSpec((None, halo, D), lambda b, i: (b, jnp.maximum(i * tb - 1, 0), 0)),
                pl.BlockSpec((None, halo, D), lambda b, i: (b, jnp.minimum((i + 1) * tb, last), 0))]

    args, specs = [x, x, x], triple()
    if proj is not None:
        o, w_o = proj
        args += [o, o, o, w_o]
        specs += triple() + [_const_spec(w_o.shape)]
    args += [g_ffn, w_in, conv_w, conv_b, w_out]
    specs += [_const_spec((1, D)), _const_spec(w_in.shape), _const_spec(conv_w.shape),
              _const_spec(conv_b.shape), _const_spec(w_out.shape)]
    if g_final is not None:
        args.append(g_final)
        specs.append(_const_spec((1, D)))
    return pl.pallas_call(
        functools.partial(_ffn_kernel, tile=T, n_chunk=n_chunk, has_proj=proj is not None,
                          final_norm=g_final is not None),
        grid=(B, S // T),
        in_specs=specs,
        out_specs=pl.BlockSpec((None, T, D), lambda b, i: (b, i, 0)),
        out_shape=jax.ShapeDtypeStruct((B, S, D), F32),
        scratch_shapes=[pltpu.VMEM((T, n_chunk * FFN_CHUNK), BF16)],
        compiler_params=_params(2),
        name="conv_ffn",
    )(*args)


def _rope_tables(S):
    half = DIFF_HEAD_DIM // 2
    pos = jnp.arange(S, dtype=F32)
    inv_freq = ROPE_THETA ** (-jnp.arange(0, DIFF_HEAD_DIM, 2, dtype=F32) / DIFF_HEAD_DIM)
    ang = pos[:, None] * inv_freq[None, :]
    cos, sin = jnp.cos(ang), jnp.sin(ang)
    cq = jnp.concatenate([cos, cos], axis=1).T
    sq = jnp.concatenate([sin, sin], axis=1).T
    ck = jnp.concatenate([cos] * 4, axis=1)
    sk = jnp.concatenate([-sin, -sin, sin, sin], axis=1)
    return cq, sq, ck, sk


def _head_feature_order():
    half = DIFF_HEAD_DIM // 2
    n = jnp.arange(DIFF_V_DIM)
    part, c, i = n // DIFF_HEAD_DIM, (n % DIFF_HEAD_DIM) // half, n % half
    within = c * DIFF_HEAD_DIM + part * half + i
    return (jnp.arange(DIFF_HEADS)[:, None] * DIFF_V_DIM + within[None, :]).reshape(-1)


def _prep_ffn(w_in, conv_w, conv_b, w_out):
    D, F2 = w_in.shape
    F = F2 // 2
    assert F % FFN_CHUNK == 0
    n = F // FFN_CHUNK

    def pair(a):
        g = a[..., :F].reshape(a.shape[:-1] + (n, FFN_CHUNK))
        u = a[..., F:].reshape(a.shape[:-1] + (n, FFN_CHUNK))
        return jnp.moveaxis(jnp.concatenate([g, u], axis=-1), -2, 0)

    return (pair(w_in).astype(BF16), pair(conv_w), pair(conv_b[None, :]),
            w_out.astype(BF16))


def _trunk(x, p):
    B, S, D = x.shape
    depth = p["norm_mix"].shape[0]
    tables = _rope_tables(S)
    pending = None
    for i in range(depth):
        j = i // 2
        g_mix = p["norm_mix"][i][None, :]
        if i % 2 == 0:
            x = _gmlp_layer(x, g_mix, *p["gmlp"][j])
        else:
            wqv_t, wk, w_o, lam, subln_g = p["diff"][j]
            qa, qb, k, vt = _qkv_rope(x, g_mix, wqv_t, wk, tables, min(K_TILE, S))
            lam_init = 0.8 - 0.6 * math.exp(-0.3 * i)
            o = _diff_attn(qa, qb, k, vt, *lam, subln_g, lam_init, min(Q_TILE, S))
            pending = (o, w_o)
        g_final = p["norm_final"][None, :] if i == depth - 1 else None
        x = _conv_ffn(x, p["norm_ffn"][i][None, :], *p["ffn"][i], proj=pending, g_final=g_final)
        pending = None
    return x


def kernel(x_prompt, x_sample, norm_mix, norm_ffn, norm_final, gmlp_w_in, gmlp_v_gain, gmlp_w_s, gmlp_b_s, gmlp_w_out, diff_w_qkv, diff_lam_q1, diff_lam_k1, diff_lam_q2, diff_lam_k2, diff_subln_g, diff_w_out, ffn_w_in, ffn_conv_w, ffn_conv_b, ffn_w_out):
    D = x_prompt.shape[-1]
    order = _head_feature_order()
    gmlp = []
    for j in range(gmlp_w_in.shape[0]):
        b_full = jnp.repeat(gmlp_b_s[j].T, CHUNK, axis=1)
        gmlp.append((gmlp_w_in[j].astype(BF16), gmlp_v_gain[j][None, :], gmlp_w_s[j].astype(BF16),
                     b_full, gmlp_w_out[j].astype(BF16)))
    diff = []
    for j in range(diff_w_qkv.shape[0]):
        w = diff_w_qkv[j]
        wq, wk, wv = w[:, :D][:, order], w[:, D:2 * D][:, order], w[:, 2 * D:]
        wqv_t = jnp.concatenate([wq, wv], axis=1).T.astype(BF16)
        lam = tuple(a[j][None, :] for a in (diff_lam_q1, diff_lam_k1, diff_lam_q2, diff_lam_k2))
        diff.append((wqv_t, wk.astype(BF16), diff_w_out[j].astype(BF16), lam,
                     diff_subln_g[j][:, None]))
    ffn = [_prep_ffn(ffn_w_in[i], ffn_conv_w[i], ffn_conv_b[i], ffn_w_out[i])
           for i in range(ffn_w_in.shape[0])]
    p = dict(norm_mix=norm_mix, norm_ffn=norm_ffn, norm_final=norm_final,
             gmlp=gmlp, diff=diff, ffn=ffn)
    return (_trunk(x_prompt, p), _trunk(x_sample, p))
```

```python
import functools
import math

import jax
import jax.numpy as jnp
from jax import lax
from jax.experimental import pallas as pl
from jax.experimental.pallas import tpu as pltpu

CHUNK = 128
GMLP_GROUPS = 8
DIFF_HEADS = 8
DIFF_HEAD_DIM = 64
DIFF_V_DIM = 2 * DIFF_HEAD_DIM
ROPE_THETA = 10000.0
NORM_EPS = 1e-6
SUBLN_EPS = 1e-5

SUBLANES = 8
MXU_WIDTH = 256
ROW_TILE = 512
GMLP_TILE = 1024
QKV_TILE = 1024
Q_TILE = 2048
K_TILE = 512
SOFTMAX_ROWS = 64
SUM_ROWS = 16
FFN_CHUNK = MXU_WIDTH
VMEM_LIMIT = 56 * 1024 * 1024

F32 = jnp.float32
BF16 = jnp.bfloat16


def _dot(a, b):
    return jnp.dot(a, b, preferred_element_type=F32)


def _rms(x, g, eps):
    ms = jnp.mean(x * x, axis=-1, keepdims=True)
    return x * lax.rsqrt(ms + eps) * g


def _const_spec(shape):
    zeros = (0,) * len(shape)
    return pl.BlockSpec(shape, lambda *_: zeros, pipeline_mode=pl.Buffered(1))


def _params(n_grid):
    return pltpu.CompilerParams(
        dimension_semantics=("arbitrary",) * n_grid, vmem_limit_bytes=VMEM_LIMIT)


def _gmlp_kernel(x_ref, g_ref, win_ref, vg_ref, ws_ref, bs_ref, wout_ref, o_ref, y_ref,
                 *, n_chunks, width):
    x = x_ref[...]
    hb = _rms(x, g_ref[...], NORM_EPS).astype(BF16)
    z = _dot(hb, win_ref[...])
    z = 0.5 * z * (1.0 + jnp.tanh(math.sqrt(2.0 / math.pi) * (z + 0.044715 * (z * z * z))))
    u = z[:, :width]
    v = _rms(z[:, width:], vg_ref[...], NORM_EPS).astype(BF16)
    for g in range(GMLP_GROUPS):
        cols = slice(g * CHUNK, (g + 1) * CHUNK)
        vcat = jnp.concatenate(
            [v[c * CHUNK:(c + 1) * CHUNK, cols] for c in range(n_chunks)], axis=1)
        r = _dot(ws_ref[g], vcat)
        b = bs_ref[:, cols]
        for c in range(n_chunks):
            rows = slice(c * CHUNK, (c + 1) * CHUNK)
            y_ref[rows, cols] = (u[rows, cols] * (r[:, c * CHUNK:(c + 1) * CHUNK] + b)).astype(BF16)
    o_ref[...] = x + _dot(y_ref[...], wout_ref[...])


def _gmlp_layer(x, g_mix, w_in, v_gain, w_s, b_full, w_out):
    B, S, D = x.shape
    W = w_out.shape[0]
    T = min(GMLP_TILE, S)
    assert S % T == 0 and T % CHUNK == 0
    row = pl.BlockSpec((None, T, D), lambda b, i: (b, i, 0))
    return pl.pallas_call(
        functools.partial(_gmlp_kernel, n_chunks=T // CHUNK, width=W),
        grid=(B, S // T),
        in_specs=[row, _const_spec((1, D)), _const_spec((D, 2 * W)), _const_spec((1, W)),
                  _const_spec((GMLP_GROUPS, CHUNK, CHUNK)), _const_spec((CHUNK, W)),
                  _const_spec((W, D))],
        out_specs=row,
        out_shape=jax.ShapeDtypeStruct((B, S, D), F32),
        scratch_shapes=[pltpu.VMEM((T, W), BF16)],
        compiler_params=_params(2),
        name="gmlp_layer",
    )(x, g_mix, w_in, v_gain, w_s, b_full, w_out)


def _qkv_kernel(x_ref, g_ref, wqv_ref, wk_ref, cq_ref, sq_ref, ck_ref, sk_ref,
                qa_ref, qb_ref, k_ref, vt_ref, *, d_model, tk):
    hb = _rms(x_ref[...], g_ref[...], NORM_EPS).astype(BF16)
    qv_t = lax.dot_general(wqv_ref[...], hb, (((1,), (1,)), ((), ())),
                           preferred_element_type=F32)
    kk = _dot(hb, wk_ref[...])
    cq, sq = cq_ref[...], sq_ref[...]
    ck, sk = ck_ref[...], sk_ref[...]
    scale = DIFF_HEAD_DIM ** -0.5 * math.log2(math.e)
    half = DIFF_HEAD_DIM // 2
    zero = jnp.zeros((half, hb.shape[0]), F32)
    ones = jnp.ones((SUM_ROWS, hb.shape[0]), F32)
    for h in range(DIFF_HEADS):
        xh = qv_t[h * DIFF_V_DIM:(h + 1) * DIFF_V_DIM]
        x1, x2 = xh[:DIFF_HEAD_DIM], xh[DIFF_HEAD_DIM:]
        o1 = (x1 * cq - x2 * sq) * scale
        o2 = (x2 * cq + x1 * sq) * scale
        qa_ref[h] = jnp.concatenate([o1[:half], zero, o2[:half], zero], axis=0).astype(BF16)
        qb_ref[h] = jnp.concatenate([zero, o1[half:], zero, o2[half:]], axis=0).astype(BF16)
        vt = jnp.concatenate(
            [qv_t[d_model + h * DIFF_V_DIM:d_model + (h + 1) * DIFF_V_DIM], ones], axis=0).astype(BF16)
        for kb in range(vt_ref.shape[1]):
            vt_ref[h, kb] = vt[:, kb * tk:(kb + 1) * tk]
        kh = kk[:, h * DIFF_V_DIM:(h + 1) * DIFF_V_DIM]
        k_ref[h] = (kh * ck + pltpu.roll(kh, DIFF_HEAD_DIM, 1) * sk).astype(BF16)


def _qkv_rope(x, g_mix, wqv_t, wk, tables, tk):
    B, S, D = x.shape
    T = min(QKV_TILE, S)
    assert S % T == 0 and T % tk == 0
    H = DIFF_HEADS
    cq, sq, ck, sk = tables
    hd = DIFF_V_DIM
    q_spec = pl.BlockSpec((None, H, hd, T), lambda b, i: (b, 0, 0, i))
    return pl.pallas_call(
        functools.partial(_qkv_kernel, d_model=D, tk=tk),
        grid=(B, S // T),
        in_specs=[pl.BlockSpec((None, T, D), lambda b, i: (b, i, 0)),
                  _const_spec((1, D)), _const_spec((2 * D, D)), _const_spec((D, D)),
                  pl.BlockSpec((DIFF_HEAD_DIM, T), lambda b, i: (0, i)),
                  pl.BlockSpec((DIFF_HEAD_DIM, T), lambda b, i: (0, i)),
                  pl.BlockSpec((T, hd), lambda b, i: (i, 0)),
                  pl.BlockSpec((T, hd), lambda b, i: (i, 0))],
        out_specs=[q_spec, q_spec,
                   pl.BlockSpec((None, H, T, hd), lambda b, i: (b, 0, i, 0)),
                   pl.BlockSpec((None, H, T // tk, hd + SUM_ROWS, tk), lambda b, i: (b, 0, i, 0, 0))],
        out_shape=[jax.ShapeDtypeStruct((B, H, hd, S), BF16),
                   jax.ShapeDtypeStruct((B, H, hd, S), BF16),
                   jax.ShapeDtypeStruct((B, H, S, hd), BF16),
                   jax.ShapeDtypeStruct((B, H, S // tk, hd + SUM_ROWS, tk), BF16)],
        compiler_params=_params(2),
        name="qkv_rope",
    )(x, g_mix, wqv_t, wk, cq, sq, ck, sk)


def _attn_kernel(qa_ref, qb_ref, k_ref, vt_ref, lq1_ref, lk1_ref, lq2_ref, lk2_ref, g_ref,
                 o_ref, sa_ref, sb_ref, pa_ref, pb_ref, acca_ref, accb_ref, *, tk, lam_init):
    qa, qb = qa_ref, qb_ref
    tq = qa.shape[1]
    n_k = k_ref.shape[0] // tk

    def k_block(i):
        return k_ref[pl.ds(pl.multiple_of(i * tk, tk), tk), :]

    def col_max(s):
        part = jnp.max(s.reshape(s.shape[0] // SUBLANES, SUBLANES, s.shape[1]), axis=0)
        return jnp.max(part, axis=0, keepdims=True)

    def scores(i, q, s_ref):
        s = _dot(k_block(i), q[...])
        s_ref[...] = s
        return col_max(s)

    def mxu_step(k_idx, q, s_ref, v_idx, p_ref, acc_ref, alpha):
        k, vt = k_block(k_idx), vt_ref[v_idx]
        mx = []
        for c in range(0, tq, MXU_WIDTH):
            cols = slice(c, c + MXU_WIDTH)
            s = _dot(k, q[:, cols])
            s_ref[:, cols] = s
            mx.append(col_max(s))
            acc_ref[:, cols] = acc_ref[:, cols] * alpha[:, cols] + _dot(vt, p_ref[:, cols])
        return jnp.concatenate(mx, axis=1)

    def softmax_step(s_ref, p_ref, m_blk, m, j):
        m_new = jnp.maximum(m, m_blk)
        zero = jnp.minimum(j, 0)
        for r in range(0, tk, SOFTMAX_ROWS):
            rows = pl.ds(pl.multiple_of(r + zero, SOFTMAX_ROWS), SOFTMAX_ROWS)
            p_ref[r:r + SOFTMAX_ROWS, :] = jnp.exp2(s_ref[rows, :] - m_new).astype(BF16)
        return m_new, jnp.exp2(m - m_new)

    def half_step(j, u, carry):
        mxa, ma, mb, alpha_b = carry
        w = 1 - u
        mxb = mxu_step(j, qb, sb_ref.at[u], jnp.maximum(j - 1, 0), pb_ref.at[w], accb_ref, alpha_b)
        ma, alpha_a = softmax_step(sa_ref.at[u], pa_ref.at[u], mxa, ma, j)
        mxa = mxu_step(jnp.minimum(j + 1, n_k - 1), qa, sa_ref.at[w], j, pa_ref.at[u], acca_ref, alpha_a)
        mb, alpha_b = softmax_step(sb_ref.at[u], pb_ref.at[u], mxb, mb, j)
        return mxa, ma, mb, alpha_b

    mxa0 = scores(0, qa, sa_ref.at[0])
    pb_ref[1] = jnp.zeros((tk, tq), BF16)
    acca_ref[...] = jnp.zeros_like(acca_ref)
    accb_ref[...] = jnp.zeros_like(accb_ref)
    neg = jnp.full((1, tq), -jnp.inf, F32)
    _, _, _, alpha_b = lax.fori_loop(
        0, n_k // 2, lambda i, c: half_step(2 * i + 1, 1, half_step(2 * i, 0, c)),
        (mxa0, neg, neg, jnp.ones((1, tq), F32)), unroll=2)
    acc_a = acca_ref[...]
    acc_b = accb_ref[...] * alpha_b + _dot(vt_ref[n_k - 1], pb_ref[1])
    hd = DIFF_V_DIM
    la, lb = acc_a[hd:hd + 1], acc_b[hd:hd + 1]
    lam = (jnp.exp(jnp.sum(lq1_ref[...] * lk1_ref[...], axis=1, keepdims=True))
           - jnp.exp(jnp.sum(lq2_ref[...] * lk2_ref[...], axis=1, keepdims=True)) + lam_init)
    o = acc_a[:hd] * (1.0 / la) - acc_b[:hd] * (lam / lb)
    ms = jnp.mean(o * o, axis=0, keepdims=True)
    o = o * (lax.rsqrt(ms + SUBLN_EPS) * (1.0 - lam_init)) * g_ref[...]
    o_ref[...] = o.T.astype(BF16)


def _diff_attn(qa, qb, k, vt, lam_q1, lam_k1, lam_q2, lam_k2, subln_g, lam_init, tq):
    B, H, hd, S = qa.shape
    n_kb, vt_rows, tk = vt.shape[2:]
    assert n_kb % 2 == 0 and S % tq == 0
    q_spec = pl.BlockSpec((None, None, hd, tq), lambda b, h, i: (b, h, 0, i))
    lam_spec = _const_spec((1, DIFF_HEAD_DIM))
    return pl.pallas_call(
        functools.partial(_attn_kernel, tk=tk, lam_init=lam_init),
        grid=(B, H, S // tq),
        in_specs=[q_spec, q_spec,
                  pl.BlockSpec((None, None, S, hd), lambda b, h, i: (b, h, 0, 0)),
                  pl.BlockSpec((None, None, n_kb, vt_rows, tk), lambda b, h, i: (b, h, 0, 0, 0)),
                  lam_spec, lam_spec, lam_spec, lam_spec, _const_spec((hd, 1))],
        out_specs=pl.BlockSpec((None, tq, hd), lambda b, h, i: (b, i, h)),
        out_shape=jax.ShapeDtypeStruct((B, S, H * hd), BF16),
        scratch_shapes=[pltpu.VMEM((2, tk, tq), F32), pltpu.VMEM((2, tk, tq), F32),
                        pltpu.VMEM((2, tk, tq), BF16), pltpu.VMEM((2, tk, tq), BF16),
                        pltpu.VMEM((vt_rows, tq), F32), pltpu.VMEM((vt_rows, tq), F32)],
        compiler_params=_params(3),
        name="diff_attn",
    )(qa, qb, k, vt, lam_q1, lam_k1, lam_q2, lam_k2, subln_g)


def _ffn_kernel(*refs, tile, n_chunk, has_proj, final_norm):
    refs = list(refs)
    x_ref, xp_ref, xn_ref = refs[:3]
    del refs[:3]
    if has_proj:
        o_ref, op_ref, on_ref, wo_ref = refs[:4]
        del refs[:4]
    gn_ref, win_ref, cw_ref, cb_ref, wout_ref = refs[:5]
    del refs[:5]
    if final_norm:
        gf_ref = refs.pop(0)
    out_ref, y_ref = refs

    halo = SUBLANES
    i, n = pl.program_id(1), pl.num_programs(1)
    xa = jnp.concatenate([xp_ref[...], x_ref[...], xn_ref[...]], axis=0)
    if has_proj:
        oa = jnp.concatenate([op_ref[...], o_ref[...], on_ref[...]], axis=0)
        xa = xa + _dot(oa, wo_ref[...])
    h = _rms(xa, gn_ref[...], NORM_EPS)
    hb = jnp.concatenate([jnp.where(i > 0, h[:halo], 0.0), h[halo:halo + tile],
                          jnp.where(i < n - 1, h[halo + tile:], 0.0)], axis=0).astype(BF16)
    n_vr = tile // SUBLANES
    sub = lax.broadcasted_iota(jnp.int32, (1, SUBLANES, 1), 1)
    for j in range(n_chunk):
        a = _dot(hb, win_ref[j])
        cw = cw_ref[j]
        a3 = a.reshape(n_vr + 2, SUBLANES, 2 * FFN_CHUNK)
        dn = pltpu.roll(a3, 1, 1)
        up = pltpu.roll(a3, SUBLANES - 1, 1)
        a_prev = jnp.where(sub == 0, dn[0:n_vr], dn[1:n_vr + 1])
        a_next = jnp.where(sub == SUBLANES - 1, up[2:n_vr + 2], up[1:n_vr + 1])
        c = cb_ref[j] + a_prev * cw[0:1] + a3[1:n_vr + 1] * cw[1:2] + a_next * cw[2:3]
        c = c.reshape(tile, 2 * FFN_CHUNK)
        gate, lin = c[:, :FFN_CHUNK], c[:, FFN_CHUNK:]
        half_gate = 0.5 * gate
        y_ref[:, j * FFN_CHUNK:(j + 1) * FFN_CHUNK] = (
            half_gate * (1.0 + jnp.tanh(half_gate)) * lin).astype(BF16)
    out = xa[halo:halo + tile] + _dot(y_ref[...], wout_ref[...])
    if final_norm:
        out = _rms(out, gf_ref[...], NORM_EPS)
    out_ref[...] = out


def _conv_ffn(x, g_ffn, w_in, conv_w, conv_b, w_out, proj=None, g_final=None):
    B, S, D = x.shape
    T = min(ROW_TILE, S)
    halo = SUBLANES
    n_chunk = w_in.shape[0]
    assert S % T == 0 and T % halo == 0
    tb, last = T // halo, S // halo - 1

    def triple():
        return [pl.BlockSpec((None, T, D), lambda b, i: (b, i, 0)),
                pl.BlockSpec((None, halo, D), lambda b, i: (b, jnp.maximum(i * tb - 1, 0), 0)),
                pl.BlockSpec((None, halo, D), lambda b, i: (b, jnp.minimum((i + 1) * tb, last), 0))]

    args, specs = [x, x, x], triple()
    if proj is not None:
        o, w_o = proj
        args += [o, o, o, w_o]
        specs += triple() + [_const_spec(w_o.shape)]
    args += [g_ffn, w_in, conv_w, conv_b, w_out]
    specs += [_const_spec((1, D)), _const_spec(w_in.shape), _const_spec(conv_w.shape),
              _const_spec(conv_b.shape), _const_spec(w_out.shape)]
    if g_final is not None:
        args.append(g_final)
        specs.append(_const_spec((1, D)))
    return pl.pallas_call(
        functools.partial(_ffn_kernel, tile=T, n_chunk=n_chunk, has_proj=proj is not None,
                          final_norm=g_final is not None),
        grid=(B, S // T),
        in_specs=specs,
        out_specs=pl.BlockSpec((None, T, D), lambda b, i: (b, i, 0)),
        out_shape=jax.ShapeDtypeStruct((B, S, D), F32),
        scratch_shapes=[pltpu.VMEM((T, n_chunk * FFN_CHUNK), BF16)],
        compiler_params=_params(2),
        name="conv_ffn",
    )(*args)


def _rope_tables(S):
    half = DIFF_HEAD_DIM // 2
    pos = jnp.arange(S, dtype=F32)
    inv_freq = ROPE_THETA ** (-jnp.arange(0, DIFF_HEAD_DIM, 2, dtype=F32) / DIFF_HEAD_DIM)
    ang = pos[:, None] * inv_freq[None, :]
    cos, sin = jnp.cos(ang), jnp.sin(ang)
    cq = jnp.concatenate([cos, cos], axis=1).T
    sq = jnp.concatenate([sin, sin], axis=1).T
    ck = jnp.concatenate([cos] * 4, axis=1)
    sk = jnp.concatenate([-sin, -sin, sin, sin], axis=1)
    return cq, sq, ck, sk


def _head_feature_order():
    half = DIFF_HEAD_DIM // 2
    n = jnp.arange(DIFF_V_DIM)
    part, c, i = n // DIFF_HEAD_DIM, (n % DIFF_HEAD_DIM) // half, n % half
    within = c * DIFF_HEAD_DIM + part * half + i
    return (jnp.arange(DIFF_HEADS)[:, None] * DIFF_V_DIM + within[None, :]).reshape(-1)


def _prep_ffn(w_in, conv_w, conv_b, w_out):
    D, F2 = w_in.shape
    F = F2 // 2
    assert F % FFN_CHUNK == 0
    n = F // FFN_CHUNK

    def pair(a):
        g = a[..., :F].reshape(a.shape[:-1] + (n, FFN_CHUNK))
        u = a[..., F:].reshape(a.shape[:-1] + (n, FFN_CHUNK))
        return jnp.moveaxis(jnp.concatenate([g, u], axis=-1), -2, 0)

    return (pair(w_in).astype(BF16), pair(conv_w), pair(conv_b[None, :]),
            w_out.astype(BF16))


def _trunk(x, p):
    B, S, D = x.shape
    depth = p["norm_mix"].shape[0]
    tables = _rope_tables(S)
    pending = None
    for i in range(depth):
        j = i // 2
        g_mix = p["norm_mix"][i][None, :]
        if i % 2 == 0:
            x = _gmlp_layer(x, g_mix, *p["gmlp"][j])
        else:
            wqv_t, wk, w_o, lam, subln_g = p["diff"][j]
            qa, qb, k, vt = _qkv_rope(x, g_mix, wqv_t, wk, tables, min(K_TILE, S))
            lam_init = 0.8 - 0.6 * math.exp(-0.3 * i)
            o = _diff_attn(qa, qb, k, vt, *lam, subln_g, lam_init, min(Q_TILE, S))
            pending = (o, w_o)
        g_final = p["norm_final"][None, :] if i == depth - 1 else None
        x = _conv_ffn(x, p["norm_ffn"][i][None, :], *p["ffn"][i], proj=pending, g_final=g_final)
        pending = None
    return x


def kernel(x_prompt, x_sample, norm_mix, norm_ffn, norm_final, gmlp_w_in, gmlp_v_gain, gmlp_w_s, gmlp_b_s, gmlp_w_out, diff_w_qkv, diff_lam_q1, diff_lam_k1, diff_lam_q2, diff_lam_k2, diff_subln_g, diff_w_out, ffn_w_in, ffn_conv_w, ffn_conv_b, ffn_w_out):
    D = x_prompt.shape[-1]
    order = _head_feature_order()
    gmlp = []
    for j in range(gmlp_w_in.shape[0]):
        b_full = jnp.repeat(gmlp_b_s[j].T, CHUNK, axis=1)
        gmlp.append((gmlp_w_in[j].astype(BF16), gmlp_v_gain[j][None, :], gmlp_w_s[j].astype(BF16),
                     b_full, gmlp_w_out[j].astype(BF16)))
    diff = []
    for j in range(diff_w_qkv.shape[0]):
        w = diff_w_qkv[j]
        wq, wk, wv = w[:, :D][:, order], w[:, D:2 * D][:, order], w[:, 2 * D:]
        wqv_t = jnp.concatenate([wq, wv], axis=1).T.astype(BF16)
        lam = tuple(a[j][None, :] for a in (diff_lam_q1, diff_lam_k1, diff_lam_q2, diff_lam_k2))
        diff.append((wqv_t, wk.astype(BF16), diff_w_out[j].astype(BF16), lam,
                     diff_subln_g[j][:, None]))
    ffn = [_prep_ffn(ffn_w_in[i], ffn_conv_w[i], ffn_conv_b[i], ffn_w_out[i])
           for i in range(ffn_w_in.shape[0])]
    p = dict(norm_mix=norm_mix, norm_ffn=norm_ffn, norm_final=norm_final,
             gmlp=gmlp, diff=diff, ffn=ffn)
    return (_trunk(x_prompt, p), _trunk(x_sample, p))
```
